```python
import jax
import jax.numpy as jnp
from jax import lax
import numpy as np

D_MODEL = 2048
BATCH = 16
SEQ = 256
DEPTH = 4
DEC_BATCH = 8
DEC_SEQ = 4096
PAST_LEN = 256

GRID_W = 64
HEAD_DIM = 128
A_HEADS = D_MODEL // 4 // HEAD_DIM
A_KV = 2
B_HEADS = D_MODEL // 2 // HEAD_DIM
B_KV = 2
C_CH = D_MODEL // 4
MIX_W = (A_HEADS + B_HEADS) * HEAD_DIM + C_CH
CONV_W = 31
WINDOW = 128
BLOCK = 128
ROPE_BASE = 10000.0
AXIS_PAIRS = HEAD_DIM // 4
D_FF = -(-(8 * D_MODEL) // (3 * 256)) * 256
N_MOD = 6
EPS = 1e-6
A_Q_W = A_HEADS * HEAD_DIM
A_KV_W = A_KV * HEAD_DIM
B_Q_W = B_HEADS * HEAD_DIM
B_KV_W = B_KV * HEAD_DIM
C_IN_W = 2 * C_CH
IN_W = A_Q_W + 2 * A_KV_W + B_Q_W + 2 * B_KV_W + C_IN_W
IN_SPLITS = (A_Q_W,
             A_Q_W + A_KV_W,
             A_Q_W + 2 * A_KV_W,
             A_Q_W + 2 * A_KV_W + B_Q_W,
             A_Q_W + 2 * A_KV_W + B_Q_W + B_KV_W,
             A_Q_W + 2 * A_KV_W + B_Q_W + 2 * B_KV_W)

kernel_name = "hybrid_prefix_diffusion_step"


def rms_norm(x, g):
    xf = x.astype(jnp.float32)
    y = xf * lax.rsqrt(jnp.mean(xf * xf, axis=-1, keepdims=True) + EPS)
    return (y * g.astype(jnp.float32)).astype(x.dtype)


def layer_norm(x, g, b):
    xf = x.astype(jnp.float32)
    mu = jnp.mean(xf, axis=-1, keepdims=True)
    xc = xf - mu
    y = xc * lax.rsqrt(jnp.mean(xc * xc, axis=-1, keepdims=True) + EPS)
    return (y * g.astype(jnp.float32) + b.astype(jnp.float32)).astype(x.dtype)


def modulate(h, shift, scale):
    return h * (1 + scale[:, None, :]) + shift[:, None, :]


def axial_rope(n_tokens):
    rows = n_tokens // GRID_W
    t = jnp.arange(rows * GRID_W)
    row = (t // GRID_W).astype(jnp.float32)
    col = (t % GRID_W).astype(jnp.float32)
    inv = jnp.power(ROPE_BASE, -jnp.arange(AXIS_PAIRS, dtype=jnp.float32) / AXIS_PAIRS)
    ang = jnp.concatenate([row[:, None] * inv, col[:, None] * inv], axis=-1)
    return jnp.cos(ang), jnp.sin(ang)


def apply_rope(x, cos, sin):
    xf = x.astype(jnp.float32)
    x1, x2 = jnp.split(xf, 2, axis=-1)
    c = cos[None, :, None, :]
    s = sin[None, :, None, :]
    return jnp.concatenate([x1 * c - x2 * s, x2 * c + x1 * s], axis=-1).astype(x.dtype)


def sink_softmax(s, sink):
    m = jnp.maximum(jnp.max(s, axis=-1, keepdims=True), sink)
    p = jnp.exp(s - m)
    return p / (jnp.sum(p, axis=-1, keepdims=True) + jnp.exp(sink - m))


def attend_blocks(q, k, v, sink):
    b, t, h, d = q.shape
    kv = k.shape[2]
    g = h // kv
    nb = t // BLOCK
    scale = d ** -0.5
    qb = jnp.moveaxis(q.reshape(b, nb, BLOCK, kv, g, d), 1, 0)

    def one(qblk):
        s = jnp.einsum('bqkgd,bskd->bkgqs', qblk, k, preferred_element_type=jnp.float32) * scale
        if sink is None:
            p = jax.nn.softmax(s, axis=-1)
        else:
            p = sink_softmax(s, sink.astype(jnp.float32).reshape(kv, g)[None, :, :, None, None])
        return jnp.einsum('bkgqs,bskd->bqkgd', p.astype(v.dtype), v)

    out = lax.map(one, qb)
    return jnp.moveaxis(out, 0, 1).reshape(b, t, h, d)


def banded_attention_with_ctx(q, k, v, k_ctx, v_ctx, sink):
    b, t, h, d = q.shape
    kv = k.shape[2]
    g = h // kv
    nb = t // BLOCK
    scale = d ** -0.5
    qb = q.reshape(b, nb, BLOCK, kv, g, d)
    pad = ((0, 0), (BLOCK, BLOCK), (0, 0), (0, 0))
    kp = jnp.pad(k, pad).reshape(b, nb + 2, BLOCK, kv, d)
    vp = jnp.pad(v, pad).reshape(b, nb + 2, BLOCK, kv, d)
    kw = jnp.concatenate([kp[:, :-2], kp[:, 1:-1], kp[:, 2:]], axis=2)
    vw = jnp.concatenate([vp[:, :-2], vp[:, 1:-1], vp[:, 2:]], axis=2)
    s_win = jnp.einsum('bnqkgd,bnskd->bnkgqs', qb, kw, preferred_element_type=jnp.float32) * scale
    qpos = jnp.arange(BLOCK)[:, None]
    kpos = jnp.arange(3 * BLOCK)[None, :] - BLOCK
    kabs = jnp.arange(nb)[:, None, None] * BLOCK + kpos
    valid = (jnp.abs(kpos - qpos) <= WINDOW)[None] & (kabs >= 0) & (kabs < t)
    s_win = jnp.where(valid[None, :, None, None, :, :], s_win, -jnp.inf)
    s_ctx = jnp.einsum('bnqkgd,bskd->bnkgqs', qb, k_ctx, preferred_element_type=jnp.float32) * scale
    s = jnp.concatenate([s_win, s_ctx], axis=-1)
    p = sink_softmax(s, sink.astype(jnp.float32).reshape(kv, g)[None, None, :, :, None, None]).astype(v.dtype)
    p_win = p[..., :3 * BLOCK]
    p_ctx = p[..., 3 * BLOCK:]
    o = (jnp.einsum('bnkgqs,bnskd->bnqkgd', p_win, vw)
         + jnp.einsum('bnkgqs,bskd->bnqkgd', p_ctx, v_ctx))
    return o.reshape(b, t, h, d)


def conformer_conv(cz, conv_w, conv_b, ln_g, ln_b):
    u = cz[..., :C_CH] * jax.nn.sigmoid(cz[..., C_CH:])
    y = lax.conv_general_dilated(u, conv_w[:, None, :].astype(u.dtype), window_strides=(1,),
                                 padding=((CONV_W // 2, CONV_W // 2),),
                                 dimension_numbers=('NWC', 'WIO', 'NWC'),
                                 feature_group_count=C_CH)
    y = y + conv_b
    return jax.nn.silu(layer_norm(y, ln_g, ln_b))


def ada_mod(cvec, w_ada, b_ada):
    return jnp.split(jax.nn.silu(cvec) @ w_ada + b_ada, N_MOD, axis=-1)


def trunk_layer(x, mod, lw, rope=None, ctx_kv=None):
    (w_in, w_out, w_gate, w_up, w_down, g1, g2, qg_a, kg_a, qg_b, kg_b,
     sink, conv_w, conv_b, cln_g, cln_b) = lw
    sh1, sc1, gt1, sh2, sc2, gt2 = mod
    b, t, _ = x.shape
    h = modulate(rms_norm(x, g1), sh1, sc1)
    z = h @ w_in
    qa, ka, va, qb, kb, vb, cz = jnp.split(z, IN_SPLITS, axis=-1)
    qa = rms_norm(qa.reshape(b, t, A_HEADS, HEAD_DIM), qg_a)
    ka = rms_norm(ka.reshape(b, t, A_KV, HEAD_DIM), kg_a)
    va = va.reshape(b, t, A_KV, HEAD_DIM)
    qb = rms_norm(qb.reshape(b, t, B_HEADS, HEAD_DIM), qg_b)
    kb = rms_norm(kb.reshape(b, t, B_KV, HEAD_DIM), kg_b)
    vb = vb.reshape(b, t, B_KV, HEAD_DIM)
    if ctx_kv is None:
        oa = attend_blocks(qa, ka, va, sink)
        ob = attend_blocks(qb, kb, vb, None)
        new_kv = (ka, va, kb, vb)
    else:
        cos, sin = rope
        ka_c, va_c, kb_c, vb_c = ctx_kv
        qa = apply_rope(qa, cos, sin)
        ka = apply_rope(ka, cos, sin)
        qb = apply_rope(qb, cos, sin)
        kb = apply_rope(kb, cos, sin)
        oa = banded_attention_with_ctx(qa, ka, va, ka_c, va_c, sink)
        ob = attend_blocks(qb, jnp.concatenate([kb_c, kb], axis=1),
                           jnp.concatenate([vb_c, vb], axis=1), None)
        new_kv = None
    oc = conformer_conv(cz, conv_w, conv_b, cln_g, cln_b)
    mix = jnp.concatenate([oa.reshape(b, t, A_Q_W), ob.reshape(b, t, B_Q_W), oc], axis=-1)
    x = x + gt1[:, None, :] * (mix @ w_out)
    h2 = modulate(rms_norm(x, g2), sh2, sc2)
    x = x + gt2[:, None, :] * ((jax.nn.silu(h2 @ w_gate) * (h2 @ w_up)) @ w_down)
    return x, new_kv


def setup_inputs(seed: int = 0) -> dict:
    key = jax.random.key(seed)
    ks = jax.random.split(key, 32)
    f = jnp.float32

    def nrm(k, shape, scale):
        return jax.random.normal(k, shape, f) * scale

    def gain(k, shape):
        return 1.0 + 0.02 * jax.random.normal(k, shape, f)

    a_cache = (DEC_BATCH, DEPTH, PAST_LEN, A_KV, HEAD_DIM)
    b_cache = (DEC_BATCH, DEPTH, PAST_LEN, B_KV, HEAD_DIM)
    return {
        "x_prompt": nrm(ks[0], (BATCH, SEQ, D_MODEL), 1.0),
        "x_sample": nrm(ks[1], (DEC_BATCH, DEC_SEQ, D_MODEL), 1.0),
        "cache_a_k": nrm(ks[2], a_cache, 1.0),
        "cache_a_v": nrm(ks[3], a_cache, 1.0),
        "cache_b_k": nrm(ks[4], b_cache, 1.0),
        "cache_b_v": nrm(ks[5], b_cache, 1.0),
        "c": nrm(ks[6], (DEC_BATCH, D_MODEL), 1.0),
        "c_ctx": nrm(ks[7], (D_MODEL,), 1.0),
        "w_ada": nrm(ks[8], (DEPTH, D_MODEL, N_MOD * D_MODEL), 0.5 * D_MODEL ** -0.5),
        "b_ada": nrm(ks[9], (DEPTH, N_MOD * D_MODEL), 0.01),
        "w_in": nrm(ks[10], (DEPTH, D_MODEL, IN_W), D_MODEL ** -0.5),
        "w_out": nrm(ks[11], (DEPTH, MIX_W, D_MODEL), MIX_W ** -0.5),
        "w_gate": nrm(ks[12], (DEPTH, D_MODEL, D_FF), D_MODEL ** -0.5),
        "w_up": nrm(ks[13], (DEPTH, D_MODEL, D_FF), D_MODEL ** -0.5),
        "w_down": nrm(ks[14], (DEPTH, D_FF, D_MODEL), D_FF ** -0.5),
        "norm1_g": gain(ks[15], (DEPTH, D_MODEL)),
        "norm2_g": gain(ks[16], (DEPTH, D_MODEL)),
        "qnorm_a_g": gain(ks[17], (DEPTH, HEAD_DIM)),
        "knorm_a_g": gain(ks[18], (DEPTH, HEAD_DIM)),
        "qnorm_b_g": gain(ks[19], (DEPTH, HEAD_DIM)),
        "knorm_b_g": gain(ks[20], (DEPTH, HEAD_DIM)),
        "sink_a": nrm(ks[21], (DEPTH, A_HEADS), 0.5),
        "conv_w": nrm(ks[22], (DEPTH, CONV_W, C_CH), CONV_W ** -0.5),
        "conv_b": nrm(ks[23], (DEPTH, C_CH), 0.01),
        "conv_ln_g": gain(ks[24], (DEPTH, C_CH)),
        "conv_ln_b": nrm(ks[25], (DEPTH, C_CH), 0.01),
    }


def reference(x_prompt, x_sample, cache_a_k, cache_a_v, cache_b_k, cache_b_v, c, c_ctx,
              w_ada, b_ada, w_in, w_out, w_gate, w_up, w_down, norm1_g, norm2_g,
              qnorm_a_g, knorm_a_g, qnorm_b_g, knorm_b_g, sink_a,
              conv_w, conv_b, conv_ln_g, conv_ln_b):
    rope = axial_rope(x_sample.shape[1])
    y_prompt = x_prompt
    y_sample = x_sample
    ak, av, bk, bv = [], [], [], []
    for l in range(DEPTH):
        lw = (w_in[l], w_out[l], w_gate[l], w_up[l], w_down[l], norm1_g[l], norm2_g[l],
              qnorm_a_g[l], knorm_a_g[l], qnorm_b_g[l], knorm_b_g[l], sink_a[l],
              conv_w[l], conv_b[l], conv_ln_g[l], conv_ln_b[l])
        mod_ctx = ada_mod(c_ctx[None, :], w_ada[l], b_ada[l])
        mod_lat = ada_mod(c, w_ada[l], b_ada[l])
        y_prompt, kv = trunk_layer(y_prompt, mod_ctx, lw)
        ak.append(kv[0])
        av.append(kv[1])
        bk.append(kv[2])
        bv.append(kv[3])
        y_sample, _ = trunk_layer(y_sample, mod_lat, lw, rope,
                                  (cache_a_k[:, l], cache_a_v[:, l], cache_b_k[:, l], cache_b_v[:, l]))
    new_a_k = jnp.stack(ak, axis=1)
    new_a_v = jnp.stack(av, axis=1)
    new_b_k = jnp.stack(bk, axis=1)
    new_b_v = jnp.stack(bv, axis=1)
    return (y_prompt, y_sample, new_a_k, new_a_v, new_b_k, new_b_v)
```

```python
import functools

import jax
import jax.numpy as jnp
from jax import lax
from jax.experimental import pallas as pl
from jax.experimental.pallas import tpu as pltpu

F32 = jnp.float32
BF16 = jnp.bfloat16

HEAD_DIM = 128
GRID_W = 64
A_HEADS = 4
A_KV = 2
B_HEADS = 8
B_KV = 2
C_CH = 512
CONV_W = 31
WINDOW = 128
ROPE_BASE = 10000.0
N_MOD = 6
EPS = 1e-6

A_Q_W = A_HEADS * HEAD_DIM
A_KV_W = A_KV * HEAD_DIM
B_Q_W = B_HEADS * HEAD_DIM
B_KV_W = B_KV * HEAD_DIM
QKV_W = A_Q_W + 2 * A_KV_W + B_Q_W + 2 * B_KV_W
CZ_W = 2 * C_CH
COL_KA = A_Q_W
COL_VA = COL_KA + A_KV_W
COL_QB = COL_VA + A_KV_W
COL_KB = COL_QB + B_Q_W
COL_VB = COL_KB + B_KV_W

V7X_VMEM_BYTES = 64 * 1024 * 1024
MASK_VALUE = -1e30
CONV_HALO = 16
CONV_ROWS = 32


def _cparams(semantics, vmem_mb):
    assert vmem_mb * 1024 * 1024 < V7X_VMEM_BYTES
    return pltpu.CompilerParams(dimension_semantics=semantics,
                                vmem_limit_bytes=vmem_mb * 1024 * 1024)


def _rms(x, g):
    return x * lax.rsqrt(jnp.mean(x * x, axis=-1, keepdims=True) + EPS) * g


def _qk_t(q, k):
    return lax.dot_general(q, k, (((1,), (1,)), ((), ())), preferred_element_type=F32)


def _mods_kernel(c_ref, w_ref, b_ref, o_ref):
    a = jax.nn.silu(c_ref[...]).astype(BF16)
    o_ref[0] = jnp.dot(a, w_ref[0].astype(BF16), preferred_element_type=F32) + b_ref[0]


def _ada_mods(cvecs, w_ada, b_ada):
    depth, d, n = w_ada.shape
    r = cvecs.shape[0]
    tn = 1024
    return pl.pallas_call(
        _mods_kernel,
        grid=(depth, n // tn),
        in_specs=[pl.BlockSpec((r, d), lambda l, j: (0, 0)),
                  pl.BlockSpec((1, d, tn), lambda l, j: (l, 0, j)),
                  pl.BlockSpec((1, 1, tn), lambda l, j: (l, 0, j))],
        out_specs=pl.BlockSpec((1, r, tn), lambda l, j: (l, 0, j)),
        out_shape=jax.ShapeDtypeStruct((depth, r, n), F32),
        compiler_params=_cparams(("arbitrary", "arbitrary"), 40),
        name="ada_mods",
    )(cvecs, w_ada, b_ada.reshape(depth, 1, n))


def _inproj_kernel(*refs, rope, emit_kv):
    x_ref, mod_ref, g1_ref, w_ref, gains_ref = refs[:5]
    refs = refs[5:]
    if rope:
        cos_ref, sin_ref = refs[:2]
        refs = refs[2:]
    z_ref, cz_ref = refs[:2]
    if emit_kv:
        ka_ref, va_ref, kb_ref, vb_ref = refs[2:6]

    h = _rms(x_ref[...], g1_ref[...])
    h = h * (1.0 + mod_ref[0, 1:2, :]) + mod_ref[0, 0:1, :]
    hb = h.astype(BF16)
    if rope:
        cos = cos_ref[...]
        sin = sin_ref[...]
    q_scale = HEAD_DIM ** -0.5

    def normed(zh, gain_idx):
        return _rms(zh, gains_ref[gain_idx:gain_idx + 1, :])

    def rotated(zh):
        if not rope:
            return zh
        return zh * cos + pltpu.roll(zh, HEAD_DIM // 2, axis=1) * sin

    chunk = 512
    for c0 in range(0, QKV_W, chunk):
        zc = jnp.dot(hb, w_ref[:, c0:c0 + chunk], preferred_element_type=F32)
        for j in range(chunk // HEAD_DIM):
            col = c0 + j * HEAD_DIM
            zh = zc[:, j * HEAD_DIM:(j + 1) * HEAD_DIM]
            if col < COL_KA:
                out = rotated(normed(zh, 0)) * q_scale
            elif col < COL_VA:
                kn = normed(zh, 1)
                if emit_kv:
                    ka_ref[:, col - COL_KA:col - COL_KA + HEAD_DIM] = kn
                out = rotated(kn)
            elif col < COL_QB:
                if emit_kv:
                    va_ref[:, col - COL_VA:col - COL_VA + HEAD_DIM] = zh
                out = zh
            elif col < COL_KB:
                out = rotated(normed(zh, 2)) * q_scale
            elif col < COL_VB:
                kn = normed(zh, 3)
                if emit_kv:
                    kb_ref[:, col - COL_KB:col - COL_KB + HEAD_DIM] = kn
                out = rotated(kn)
            else:
                if emit_kv:
                    vb_ref[:, col - COL_VB:col - COL_VB + HEAD_DIM] = zh
                out = zh
            z_ref[:, col:col + HEAD_DIM] = out.astype(BF16)
    for c0 in range(0, CZ_W, chunk):
        cz_ref[:, c0:c0 + chunk] = jnp.dot(hb, w_ref[:, QKV_W + c0:QKV_W + c0 + chunk],
                                           preferred_element_type=F32)


def _in_projection(x, mod, g1, w_in, gains, rope_tabs, seq_len, emit_kv, tm):
    n, d = x.shape
    tm = min(tm, seq_len)
    tiles_per_seq = seq_len // tm
    nb = mod.shape[0]
    if nb == 1:
        mod_map = lambda i: (0, 0, 0)
    else:
        mod_map = lambda i: (i // tiles_per_seq, 0, 0)
    const = lambda i: (0, 0)
    in_specs = [pl.BlockSpec((tm, d), lambda i: (i, 0)),
                pl.BlockSpec((1, N_MOD, d), mod_map),
                pl.BlockSpec((1, d), const),
                pl.BlockSpec(w_in.shape, const, pipeline_mode=pl.Buffered(1)),
                pl.BlockSpec(gains.shape, const)]
    args = [x, mod, g1, w_in, gains]
    rope = rope_tabs is not None
    if rope:
        in_specs += [pl.BlockSpec((tm, HEAD_DIM), lambda i: (i % tiles_per_seq, 0))] * 2
        args += list(rope_tabs)
    out_specs = [pl.BlockSpec((tm, QKV_W), lambda i: (i, 0)),
                 pl.BlockSpec((tm, CZ_W), lambda i: (i, 0))]
    out_shape = [jax.ShapeDtypeStruct((n, QKV_W), BF16), jax.ShapeDtypeStruct((n, CZ_W), F32)]
    if emit_kv:
        out_specs += [pl.BlockSpec((tm, A_KV_W), lambda i: (i, 0))] * 4
        out_shape += [jax.ShapeDtypeStruct((n, A_KV_W), F32)] * 4
    return pl.pallas_call(
        functools.partial(_inproj_kernel, rope=rope, emit_kv=emit_kv),
        grid=(n // tm,),
        in_specs=in_specs,
        out_specs=out_specs,
        out_shape=out_shape,
        compiler_params=_cparams(("arbitrary",), 56),
        name="in_projection_lat" if rope else "in_projection_ctx",
    )(*args)


def _softmax_pv(s_list, v_list, sink_col):
    m = s_list[0].max(axis=-1, keepdims=True)
    for s in s_list[1:]:
        m = jnp.maximum(m, s.max(axis=-1, keepdims=True))
    if sink_col is not None:
        m = jnp.maximum(m, sink_col)
    denom = None
    acc = None
    for s, v in zip(s_list, v_list):
        p = jnp.exp(s - m)
        ps = p.sum(axis=-1, keepdims=True)
        pv = jnp.dot(p.astype(BF16), v, preferred_element_type=F32)
        denom = ps if denom is None else denom + ps
        acc = pv if acc is None else acc + pv
    if sink_col is not None:
        denom = denom + jnp.exp(sink_col - m)
    return acc * (1.0 / denom)


def _stack_heads(ref, col0, n_heads):
    return jnp.concatenate([ref[:, col0 + g * HEAD_DIM:col0 + (g + 1) * HEAD_DIM]
                            for g in range(n_heads)], axis=0)


def _sink_column(sink_ref, head0, n_heads, rows_per_head):
    row = lax.broadcasted_iota(jnp.int32, (n_heads * rows_per_head, 1), 0)
    col = jnp.full((n_heads * rows_per_head, 1), sink_ref[head0], F32)
    for g in range(1, n_heads):
        col = jnp.where(row >= g * rows_per_head, sink_ref[head0 + g], col)
    return col


def _ctx_attn_kernel(sink_ref, z_ref, oa_ref, ob_ref):
    t = z_ref.shape[0]
    ga = A_HEADS // A_KV
    for kh in range(A_KV):
        q = _stack_heads(z_ref, kh * ga * HEAD_DIM, ga)
        k = z_ref[:, COL_KA + kh * HEAD_DIM:COL_KA + (kh + 1) * HEAD_DIM]
        v = z_ref[:, COL_VA + kh * HEAD_DIM:COL_VA + (kh + 1) * HEAD_DIM]
        o = _softmax_pv([_qk_t(q, k)], [v], _sink_column(sink_ref, kh * ga, ga, t))
        for g in range(ga):
            c = (kh * ga + g) * HEAD_DIM
            oa_ref[:, c:c + HEAD_DIM] = o[g * t:(g + 1) * t].astype(BF16)
    gb = B_HEADS // B_KV
    for kh in range(B_KV):
        q = _stack_heads(z_ref, COL_QB + kh * gb * HEAD_DIM, gb)
        k = z_ref[:, COL_KB + kh * HEAD_DIM:COL_KB + (kh + 1) * HEAD_DIM]
        v = z_ref[:, COL_VB + kh * HEAD_DIM:COL_VB + (kh + 1) * HEAD_DIM]
        o = _softmax_pv([_qk_t(q, k)], [v], None)
        for g in range(gb):
            c = (kh * gb + g) * HEAD_DIM
            ob_ref[:, c:c + HEAD_DIM] = o[g * t:(g + 1) * t].astype(BF16)


def _ctx_attention(z, sink, seq_len):
    n = z.shape[0]
    return pl.pallas_call(
        _ctx_attn_kernel,
        grid=(n // seq_len,),
        in_specs=[pl.BlockSpec(memory_space=pltpu.SMEM),
                  pl.BlockSpec((seq_len, QKV_W), lambda b: (b, 0))],
        out_specs=[pl.BlockSpec((seq_len, A_Q_W), lambda b: (b, 0)),
                   pl.BlockSpec((seq_len, B_Q_W), lambda b: (b, 0))],
        out_shape=[jax.ShapeDtypeStruct((n, A_Q_W), BF16), jax.ShapeDtypeStruct((n, B_Q_W), BF16)],
        compiler_params=_cparams(("arbitrary",), 32),
        name="ctx_attention",
    )(sink, z)


def _win_attn_kernel(sink_ref, q_ref, kp_ref, kc_ref, kn_ref, vp_ref, vc_ref, vn_ref,
                     kx_ref, vx_ref, oa_ref, *, seq_len):
    tq = q_ref.shape[0]
    t0 = pl.program_id(1) * tq
    ga = A_HEADS // A_KV
    nk = tq + 2 * WINDOW
    row = lax.broadcasted_iota(jnp.int32, (ga * tq, nk), 0) & (tq - 1)
    col = lax.broadcasted_iota(jnp.int32, (ga * tq, nk), 1) - WINDOW
    key_pos = col + t0
    valid = (jnp.abs(col - row) <= WINDOW) & (key_pos >= 0) & (key_pos < seq_len)
    for kh in range(A_KV):
        hs = slice(kh * HEAD_DIM, (kh + 1) * HEAD_DIM)
        q = _stack_heads(q_ref, kh * ga * HEAD_DIM, ga)
        kw = jnp.concatenate([kp_ref[:, hs], kc_ref[:, hs], kn_ref[:, hs]], axis=0)
        vw = jnp.concatenate([vp_ref[:, hs], vc_ref[:, hs], vn_ref[:, hs]], axis=0)
        s_win = jnp.where(valid, _qk_t(q, kw), MASK_VALUE)
        s_ctx = _qk_t(q, kx_ref[0, :, hs])
        o = _softmax_pv([s_win, s_ctx], [vw, vx_ref[0, :, hs]],
                        _sink_column(sink_ref, kh * ga, ga, tq))
        for g in range(ga):
            c = (kh * ga + g) * HEAD_DIM
            oa_ref[:, c:c + HEAD_DIM] = o[g * tq:(g + 1) * tq].astype(BF16)


def _win_attention(z, kx, vx, sink, seq_len, tq):
    n = z.shape[0]
    nb = n // seq_len
    tq = min(tq, seq_len)
    assert tq & (tq - 1) == 0 and tq % WINDOW == 0
    nt = seq_len // tq
    halos_per_tile = tq // WINDOW
    n_halo = n // WINDOW
    past = kx.shape[1]
    cur = lambda cb: pl.BlockSpec((tq, A_KV_W), lambda b, t: (b * nt + t, cb))
    prev = lambda cb: pl.BlockSpec(
        (WINDOW, A_KV_W), lambda b, t: (jnp.maximum((b * nt + t) * halos_per_tile - 1, 0), cb))
    nxt = lambda cb: pl.BlockSpec(
        (WINDOW, A_KV_W),
        lambda b, t: (jnp.minimum((b * nt + t + 1) * halos_per_tile, n_halo - 1), cb))
    kcb = COL_KA // A_KV_W
    vcb = COL_VA // A_KV_W
    ctx = pl.BlockSpec((1, past, A_KV_W), lambda b, t: (b, 0, 0))
    return pl.pallas_call(
        functools.partial(_win_attn_kernel, seq_len=seq_len),
        grid=(nb, nt),
        in_specs=[pl.BlockSpec(memory_space=pltpu.SMEM),
                  pl.BlockSpec((tq, A_Q_W), lambda b, t: (b * nt + t, 0)),
                  prev(kcb), cur(kcb), nxt(kcb), prev(vcb), cur(vcb), nxt(vcb), ctx, ctx],
        out_specs=pl.BlockSpec((tq, A_Q_W), lambda b, t: (b * nt + t, 0)),
        out_shape=jax.ShapeDtypeStruct((n, A_Q_W), BF16),
        compiler_params=_cparams(("arbitrary", "arbitrary"), 40),
        name="win_attention",
    )(sink, z, z, z, z, z, z, z, kx, vx)


def _glob_attn_kernel(q_ref, kx_ref, vx_ref, kl_ref, vl_ref, ob_ref, *, tk):
    tq = q_ref.shape[0]
    gb = B_HEADS // B_KV
    q = _stack_heads(q_ref, 0, gb)

    s = _qk_t(q, kx_ref[0])
    m = s.max(axis=-1, keepdims=True)
    p = jnp.exp(s - m)
    l = p.sum(axis=-1, keepdims=True)
    acc = jnp.dot(p.astype(BF16), vx_ref[0], preferred_element_type=F32)
    for c0 in range(0, kl_ref.shape[0], tk):
        s = _qk_t(q, kl_ref[c0:c0 + tk, :])
        m_new = jnp.maximum(m, s.max(axis=-1, keepdims=True))
        alpha = jnp.exp(m - m_new)
        p = jnp.exp(s - m_new)
        l = alpha * l + p.sum(axis=-1, keepdims=True)
        acc = alpha * acc + jnp.dot(p.astype(BF16), vl_ref[c0:c0 + tk, :],
                                    preferred_element_type=F32)
        m = m_new
    o = acc * (1.0 / l)
    for g in range(gb):
        ob_ref[:, g * HEAD_DIM:(g + 1) * HEAD_DIM] = o[g * tq:(g + 1) * tq].astype(BF16)


def _glob_attention(z, kx, vx, seq_len, tq, tk):
    n = z.shape[0]
    nb = n // seq_len
    tq = min(tq, seq_len)
    tk = min(tk, seq_len)
    nt = seq_len // tq
    past = kx.shape[1]
    q_w = B_Q_W // B_KV
    ctx = pl.BlockSpec((1, past, HEAD_DIM), lambda b, kh, t: (b, 0, kh))
    return pl.pallas_call(
        functools.partial(_glob_attn_kernel, tk=tk),
        grid=(nb, B_KV, nt),
        in_specs=[pl.BlockSpec((tq, q_w), lambda b, kh, t: (b * nt + t, COL_QB // q_w + kh)),
                  ctx, ctx,
                  pl.BlockSpec((seq_len, HEAD_DIM), lambda b, kh, t: (b, COL_KB // HEAD_DIM + kh)),
                  pl.BlockSpec((seq_len, HEAD_DIM), lambda b, kh, t: (b, COL_VB // HEAD_DIM + kh))],
        out_specs=pl.BlockSpec((tq, q_w), lambda b, kh, t: (b * nt + t, kh)),
        out_shape=jax.ShapeDtypeStruct((n, B_Q_W), BF16),
        compiler_params=_cparams(("arbitrary", "arbitrary", "arbitrary"), 48),
        name="glob_attention",
    )(z, kx, vx, z, z)


def _conv_kernel(cp_ref, cc_ref, cn_ref, w_ref, b_ref, g_ref, beta_ref, o_ref, u_ref, win_ref, *,
                 tiles_per_seq):
    tc = cc_ref.shape[0]
    ti = pl.program_id(0) % tiles_per_seq

    def glu(ref):
        return ref[:, :C_CH] * jax.nn.sigmoid(ref[:, C_CH:])

    u_ref[0:CONV_HALO, :] = jnp.where(ti > 0, glu(cp_ref), 0.0)
    u_ref[CONV_HALO:CONV_HALO + tc, :] = glu(cc_ref)
    u_ref[CONV_HALO + tc:, :] = jnp.where(ti < tiles_per_seq - 1, glu(cn_ref), 0.0)

    bias = b_ref[...]
    gamma = g_ref[...]
    beta = beta_ref[...]
    base = CONV_HALO - CONV_W // 2

    def body(r, carry):
        r0 = pl.multiple_of(r * CONV_ROWS, CONV_ROWS)
        win_ref[...] = u_ref[pl.ds(r0, 2 * CONV_ROWS), :]
        y = w_ref[0:1, :] * win_ref[base:base + CONV_ROWS, :]
        for k in range(1, CONV_W):
            y = y + w_ref[k:k + 1, :] * win_ref[base + k:base + k + CONV_ROWS, :]
        y = y + bias
        mu = jnp.mean(y, axis=-1, keepdims=True)
        yc = y - mu
        yn = yc * lax.rsqrt(jnp.mean(yc * yc, axis=-1, keepdims=True) + EPS) * gamma + beta
        o_ref[pl.ds(r0, CONV_ROWS), :] = jax.nn.silu(yn).astype(BF16)
        return carry

    lax.fori_loop(0, tc // CONV_ROWS, body, 0)


def _conformer_conv(cz, conv_w, conv_b, ln_g, ln_b, seq_len, tc):
    n = cz.shape[0]
    tc = min(tc, seq_len)
    tiles_per_seq = seq_len // tc
    halos_per_tile = tc // CONV_HALO
    n_halo = n // CONV_HALO
    const = lambda i: (0, 0)
    return pl.pallas_call(
        functools.partial(_conv_kernel, tiles_per_seq=tiles_per_seq),
        grid=(n // tc,),
        in_specs=[pl.BlockSpec((CONV_HALO, CZ_W),
                               lambda i: (jnp.maximum(i * halos_per_tile - 1, 0), 0)),
                  pl.BlockSpec((tc, CZ_W), lambda i: (i, 0)),
                  pl.BlockSpec((CONV_HALO, CZ_W),
                               lambda i: (jnp.minimum((i + 1) * halos_per_tile, n_halo - 1), 0)),
                  pl.BlockSpec((CONV_W, C_CH), const),
                  pl.BlockSpec((1, C_CH), const),
                  pl.BlockSpec((1, C_CH), const),
                  pl.BlockSpec((1, C_CH), const)],
        out_specs=pl.BlockSpec((tc, C_CH), lambda i: (i, 0)),
        out_shape=jax.ShapeDtypeStruct((n, C_CH), BF16),
        scratch_shapes=[pltpu.VMEM((tc + 2 * CONV_HALO, C_CH), F32),
                        pltpu.VMEM((2 * CONV_ROWS, C_CH), F32)],
        compiler_params=_cparams(("arbitrary",), 32),
        name="conformer_conv",
    )(cz, cz, cz, conv_w, conv_b, ln_g, ln_b)


def _outproj_kernel(oa_ref, ob_ref, oc_ref, x_ref, mod_ref, g2_ref, w_ref, xo_ref, h2_ref):
    y = jnp.dot(oa_ref[...], w_ref[0:A_Q_W, :], preferred_element_type=F32)
    y = y + jnp.dot(ob_ref[...], w_ref[A_Q_W:A_Q_W + B_Q_W, :], preferred_element_type=F32)
    y = y + jnp.dot(oc_ref[...], w_ref[A_Q_W + B_Q_W:, :], preferred_element_type=F32)
    x = x_ref[...] + mod_ref[0, 2:3, :] * y
    xo_ref[...] = x
    h2 = _rms(x, g2_ref[...])
    h2_ref[...] = (h2 * (1.0 + mod_ref[0, 4:5, :]) + mod_ref[0, 3:4, :]).astype(BF16)


def _out_projection(oa, ob, oc, x, mod, g2, w_out, seq_len, tm):
    n, d = x.shape
    tm = min(tm, seq_len)
    tiles_per_seq = seq_len // tm
    if mod.shape[0] == 1:
        mod_map = lambda i: (0, 0, 0)
    else:
        mod_map = lambda i: (i // tiles_per_seq, 0, 0)
    const = lambda i: (0, 0)
    row = lambda w: pl.BlockSpec((tm, w), lambda i: (i, 0))
    return pl.pallas_call(
        _outproj_kernel,
        grid=(n // tm,),
        in_specs=[row(A_Q_W), row(B_Q_W), row(C_CH), row(d),
                  pl.BlockSpec((1, N_MOD, d), mod_map),
                  pl.BlockSpec((1, d), const),
                  pl.BlockSpec(w_out.shape, const, pipeline_mode=pl.Buffered(1))],
        out_specs=[row(d), row(d)],
        out_shape=[jax.ShapeDtypeStruct((n, d), F32), jax.ShapeDtypeStruct((n, d), BF16)],
        compiler_params=_cparams(("arbitrary",), 56),
        name="out_projection",
    )(oa, ob, oc, x, mod, g2, w_out)


def _ffn_kernel(h_ref, wg_ref, wu_ref, wd_ref, x_ref, mod_ref, o_ref, acc_ref):
    f = pl.program_id(1)
    h = h_ref[...]
    g = jnp.dot(h, wg_ref[...], preferred_element_type=F32)
    u = jnp.dot(h, wu_ref[...], preferred_element_type=F32)
    a = (jax.nn.silu(g) * u).astype(BF16)
    part = jnp.dot(a, wd_ref[...], preferred_element_type=F32)

    @pl.when(f == 0)
    def _():
        acc_ref[...] = part

    @pl.when(f > 0)
    def _():
        acc_ref[...] += part

    @pl.when(f == pl.num_programs(1) - 1)
    def _():
        o_ref[...] = x_ref[...] + mod_ref[0, 5:6, :] * acc_ref[...]


def _ffn(h2, w_gate, w_up, w_down, x, mod, seq_len, tm, tf):
    n, d = x.shape
    d_ff = w_gate.shape[1]
    tm = min(tm, seq_len)
    tiles_per_seq = seq_len // tm
    if mod.shape[0] == 1:
        mod_map = lambda i, f: (0, 0, 0)
    else:
        mod_map = lambda i, f: (i // tiles_per_seq, 0, 0)
    return pl.pallas_call(
        _ffn_kernel,
        grid=(n // tm, d_ff // tf),
        in_specs=[pl.BlockSpec((tm, d), lambda i, f: (i, 0)),
                  pl.BlockSpec((d, tf), lambda i, f: (0, f)),
                  pl.BlockSpec((d, tf), lambda i, f: (0, f)),
                  pl.BlockSpec((tf, d), lambda i, f: (f, 0)),
                  pl.BlockSpec((tm, d), lambda i, f: (i, 0)),
                  pl.BlockSpec((1, N_MOD, d), mod_map)],
        out_specs=pl.BlockSpec((tm, d), lambda i, f: (i, 0)),
        out_shape=jax.ShapeDtypeStruct((n, d), F32),
        scratch_shapes=[pltpu.VMEM((tm, d), F32)],
        compiler_params=_cparams(("arbitrary", "arbitrary"), 56),
        name="swiglu_ffn",
    )(h2, w_gate, w_up, w_down, x, mod)


def _rope_tables(n_tokens):
    pairs = HEAD_DIM // 4
    t = jnp.arange(n_tokens)
    row = (t // GRID_W).astype(F32)
    col = (t % GRID_W).astype(F32)
    inv = jnp.power(ROPE_BASE, -jnp.arange(pairs, dtype=F32) / pairs)
    ang = jnp.concatenate([row[:, None] * inv, col[:, None] * inv], axis=-1)
    cos = jnp.cos(ang)
    sin = jnp.sin(ang)
    return jnp.concatenate([cos, cos], axis=-1), jnp.concatenate([-sin, sin], axis=-1)


def _trunk_layer(x, mod, lw, seq_len, rope_tabs=None, ctx_kv=None):
    (w_in, w_out, w_gate, w_up, w_down, g1, g2, gains, sink, conv_w, conv_b, cln_g, cln_b) = lw
    latent = ctx_kv is not None
    proj = _in_projection(x, mod, g1, w_in, gains, rope_tabs, seq_len, emit_kv=not latent, tm=512)
    z, cz = proj[0], proj[1]
    if latent:
        ka_c, va_c, kb_c, vb_c = ctx_kv
        oa = _win_attention(z, ka_c, va_c, sink, seq_len, tq=512)
        ob = _glob_attention(z, kb_c, vb_c, seq_len, tq=256, tk=1024)
    else:
        oa, ob = _ctx_attention(z, sink, seq_len)
    oc = _conformer_conv(cz, conv_w, conv_b, cln_g, cln_b, seq_len, tc=512)
    x, h2 = _out_projection(oa, ob, oc, x, mod, g2, w_out, seq_len, tm=512)
    x = _ffn(h2, w_gate, w_up, w_down, x, mod, seq_len, tm=512, tf=512)
    return x, proj[2:]


def kernel(x_prompt, x_sample, cache_a_k, cache_a_v, cache_b_k, cache_b_v, c, c_ctx, w_ada, b_ada, w_in, w_out, w_gate, w_up, w_down, norm1_g, norm2_g, qnorm_a_g, knorm_a_g, qnorm_b_g, knorm_b_g, sink_a, conv_w, conv_b, conv_ln_g, conv_ln_b):
    batch, seq, d = x_prompt.shape
    dec_batch, dec_seq, _ = x_sample.shape
    depth = w_in.shape[0]
    past = cache_a_k.shape[2]

    n_rows = 1 + dec_batch
    pad_rows = -n_rows % 8
    cvecs = jnp.concatenate([c_ctx[None, :], c, jnp.zeros((pad_rows, d), F32)], axis=0)
    mods = _ada_mods(cvecs, w_ada, b_ada).reshape(depth, n_rows + pad_rows, N_MOD, d)

    rope_tabs = _rope_tables(dec_seq)
    w_in_b, w_out_b = w_in.astype(BF16), w_out.astype(BF16)
    w_gate_b, w_up_b, w_down_b = w_gate.astype(BF16), w_up.astype(BF16), w_down.astype(BF16)
    ctx_kv = [t.astype(BF16).reshape(dec_batch, depth, past, -1)
              for t in (cache_a_k, cache_a_v, cache_b_k, cache_b_v)]

    y_prompt = x_prompt.reshape(batch * seq, d)
    y_sample = x_sample.reshape(dec_batch * dec_seq, d)
    new_kv = []
    for l in range(depth):
        gains = jnp.stack([qnorm_a_g[l], knorm_a_g[l], qnorm_b_g[l], knorm_b_g[l]])
        lw = (w_in_b[l], w_out_b[l], w_gate_b[l], w_up_b[l], w_down_b[l],
              norm1_g[l][None, :], norm2_g[l][None, :], gains, sink_a[l],
              conv_w[l], conv_b[l][None, :], conv_ln_g[l][None, :], conv_ln_b[l][None, :])
        y_prompt, kv = _trunk_layer(y_prompt, mods[l, 0:1], lw, seq)
        new_kv.append(kv)
        y_sample, _ = _trunk_layer(y_sample, mods[l, 1:n_rows], lw, dec_seq, rope_tabs,
                                   [t[:, l] for t in ctx_kv])

    def stacked(i, n_kv):
        return jnp.stack([kv[i].reshape(batch, seq, n_kv, HEAD_DIM) for kv in new_kv], axis=1)

    return (y_prompt.reshape(batch, seq, d), y_sample.reshape(dec_batch, dec_seq, d),
            stacked(0, A_KV), stacked(1, A_KV), stacked(2, B_KV), stacked(3, B_KV))
```

```python
import functools

import jax
import jax.numpy as jnp
from jax import lax
from jax.experimental import pallas as pl
from jax.experimental.pallas import tpu as pltpu

F32 = jnp.float32
BF16 = jnp.bfloat16

HEAD_DIM = 128
GRID_W = 64
A_HEADS = 4
A_KV = 2
B_HEADS = 8
B_KV = 2
C_CH = 512
CONV_W = 31
WINDOW = 128
ROPE_BASE = 10000.0
N_MOD = 6
EPS = 1e-6

A_Q_W = A_HEADS * HEAD_DIM
A_KV_W = A_KV * HEAD_DIM
B_Q_W = B_HEADS * HEAD_DIM
B_KV_W = B_KV * HEAD_DIM
QKV_W = A_Q_W + 2 * A_KV_W + B_Q_W + 2 * B_KV_W
CZ_W = 2 * C_CH
COL_KA = A_Q_W
COL_VA = COL_KA + A_KV_W
COL_QB = COL_VA + A_KV_W
COL_KB = COL_QB + B_Q_W
COL_VB = COL_KB + B_KV_W

V7X_VMEM_BYTES = 64 * 1024 * 1024
MASK_VALUE = -1e30
CONV_HALO = 16
CONV_ROWS = 32
CONV_UNROLL = 4
SUBLANES = 8


def _cparams(semantics, vmem_mb):
    assert vmem_mb * 1024 * 1024 < V7X_VMEM_BYTES
    return pltpu.CompilerParams(dimension_semantics=semantics,
                                vmem_limit_bytes=vmem_mb * 1024 * 1024)


def _rms(x, g):
    return x * lax.rsqrt(jnp.mean(x * x, axis=-1, keepdims=True) + EPS) * g


def _qk_t(q, k):
    return lax.dot_general(q, k, (((1,), (1,)), ((), ())), preferred_element_type=F32)


def _token_tiling(n_tokens, seq_len, n_mod_rows, tm):
    span = n_tokens if n_mod_rows == 1 else seq_len
    tm = min(tm, span)
    assert span % tm == 0
    return tm, span // tm


def _mods_kernel(c_ref, w_ref, b_ref, o_ref):
    a = jax.nn.silu(c_ref[...]).astype(BF16)
    o_ref[0] = jnp.dot(a, w_ref[0].astype(BF16), preferred_element_type=F32) + b_ref[0]


def _ada_mods(cvecs, w_ada, b_ada):
    depth, d, n = w_ada.shape
    r = cvecs.shape[0]
    tn = 1024
    return pl.pallas_call(
        _mods_kernel,
        grid=(depth, n // tn),
        in_specs=[pl.BlockSpec((r, d), lambda l, j: (0, 0)),
                  pl.BlockSpec((1, d, tn), lambda l, j: (l, 0, j)),
                  pl.BlockSpec((1, 1, tn), lambda l, j: (l, 0, j))],
        out_specs=pl.BlockSpec((1, r, tn), lambda l, j: (l, 0, j)),
        out_shape=jax.ShapeDtypeStruct((depth, r, n), F32),
        compiler_params=_cparams(("arbitrary", "arbitrary"), 40),
        name="ada_mods",
    )(cvecs, w_ada, b_ada.reshape(depth, 1, n))


def _inproj_kernel(*refs, rope, emit_kv):
    x_ref, mod_ref, g1_ref, w_ref, gains_ref = refs[:5]
    refs = refs[5:]
    if rope:
        cos_ref, sin_ref = refs[:2]
        refs = refs[2:]
    z_ref, cz_ref = refs[:2]
    if emit_kv:
        ka_ref, va_ref, kb_ref, vb_ref = refs[2:6]

    h = _rms(x_ref[...], g1_ref[...])
    h = h * (1.0 + mod_ref[0, 1:2, :]) + mod_ref[0, 0:1, :]
    hb = h.astype(BF16)
    if rope:
        cos = cos_ref[...]
        sin = sin_ref[...]
    q_scale = HEAD_DIM ** -0.5

    def normed(zh, gain_idx):
        return _rms(zh, gains_ref[gain_idx:gain_idx + 1, :])

    def rotated(zh):
        if not rope:
            return zh
        return zh * cos + pltpu.roll(zh, HEAD_DIM // 2, axis=1) * sin

    chunk = 512
    for c0 in range(0, QKV_W, chunk):
        zc = jnp.dot(hb, w_ref[:, c0:c0 + chunk], preferred_element_type=F32)
        for j in range(chunk // HEAD_DIM):
            col = c0 + j * HEAD_DIM
            zh = zc[:, j * HEAD_DIM:(j + 1) * HEAD_DIM]
            if col < COL_KA:
                out = rotated(normed(zh, 0)) * q_scale
            elif col < COL_VA:
                kn = normed(zh, 1)
                if emit_kv:
                    ka_ref[:, col - COL_KA:col - COL_KA + HEAD_DIM] = kn
                out = rotated(kn)
            elif col < COL_QB:
                if emit_kv:
                    va_ref[:, col - COL_VA:col - COL_VA + HEAD_DIM] = zh
                out = zh
            elif col < COL_KB:
                out = rotated(normed(zh, 2)) * q_scale
            elif col < COL_VB:
                kn = normed(zh, 3)
                if emit_kv:
                    kb_ref[:, col - COL_KB:col - COL_KB + HEAD_DIM] = kn
                out = rotated(kn)
            else:
                if emit_kv:
                    vb_ref[:, col - COL_VB:col - COL_VB + HEAD_DIM] = zh
                out = zh
            z_ref[:, col:col + HEAD_DIM] = out.astype(BF16)
    for c0 in range(0, CZ_W, chunk):
        cz_ref[:, c0:c0 + chunk] = jnp.dot(hb, w_ref[:, QKV_W + c0:QKV_W + c0 + chunk],
                                           preferred_element_type=F32)


def _in_projection(x, mod, g1, w_in, gains, rope_tabs, seq_len, emit_kv, tm):
    n, d = x.shape
    tm, tiles_per_seq = _token_tiling(n, seq_len, mod.shape[0], tm)
    mod_map = lambda i: (i // tiles_per_seq, 0, 0)
    const = lambda i: (0, 0)
    in_specs = [pl.BlockSpec((tm, d), lambda i: (i, 0)),
                pl.BlockSpec((1, N_MOD, d), mod_map),
                pl.BlockSpec((1, d), const),
                pl.BlockSpec(w_in.shape, const, pipeline_mode=pl.Buffered(1)),
                pl.BlockSpec(gains.shape, const)]
    args = [x, mod, g1, w_in, gains]
    rope = rope_tabs is not None
    if rope:
        in_specs += [pl.BlockSpec((tm, HEAD_DIM), lambda i: (i % tiles_per_seq, 0))] * 2
        args += list(rope_tabs)
    out_specs = [pl.BlockSpec((tm, QKV_W), lambda i: (i, 0)),
                 pl.BlockSpec((tm, CZ_W), lambda i: (i, 0))]
    out_shape = [jax.ShapeDtypeStruct((n, QKV_W), BF16), jax.ShapeDtypeStruct((n, CZ_W), F32)]
    if emit_kv:
        out_specs += [pl.BlockSpec((tm, A_KV_W), lambda i: (i, 0))] * 4
        out_shape += [jax.ShapeDtypeStruct((n, A_KV_W), F32)] * 4
    return pl.pallas_call(
        functools.partial(_inproj_kernel, rope=rope, emit_kv=emit_kv),
        grid=(n // tm,),
        in_specs=in_specs,
        out_specs=out_specs,
        out_shape=out_shape,
        compiler_params=_cparams(("arbitrary",), 56),
        name="in_projection_lat" if rope else "in_projection_ctx",
    )(*args)


def _softmax_pv(s_list, v_list, sink_col):
    m = s_list[0].max(axis=-1, keepdims=True)
    for s in s_list[1:]:
        m = jnp.maximum(m, s.max(axis=-1, keepdims=True))
    if sink_col is not None:
        m = jnp.maximum(m, sink_col)
    denom = None
    acc = None
    for s, v in zip(s_list, v_list):
        p = jnp.exp(s - m)
        ps = p.sum(axis=-1, keepdims=True)
        pv = jnp.dot(p.astype(BF16), v, preferred_element_type=F32)
        denom = ps if denom is None else denom + ps
        acc = pv if acc is None else acc + pv
    if sink_col is not None:
        denom = denom + jnp.exp(sink_col - m)
    return acc * (1.0 / denom)


def _stack_heads(ref, col0, n_heads):
    return jnp.concatenate([ref[:, col0 + g * HEAD_DIM:col0 + (g + 1) * HEAD_DIM]
                            for g in range(n_heads)], axis=0)


def _sink_column(sink_ref, head0, n_heads, rows_per_head):
    row = lax.broadcasted_iota(jnp.int32, (n_heads * rows_per_head, 1), 0)
    col = jnp.full((n_heads * rows_per_head, 1), sink_ref[head0], F32)
    for g in range(1, n_heads):
        col = jnp.where(row >= g * rows_per_head, sink_ref[head0 + g], col)
    return col


def _ctx_attn_kernel(sink_ref, z_ref, oa_ref, ob_ref):
    t = z_ref.shape[0]
    ga = A_HEADS // A_KV
    for kh in range(A_KV):
        q = _stack_heads(z_ref, kh * ga * HEAD_DIM, ga)
        k = z_ref[:, COL_KA + kh * HEAD_DIM:COL_KA + (kh + 1) * HEAD_DIM]
        v = z_ref[:, COL_VA + kh * HEAD_DIM:COL_VA + (kh + 1) * HEAD_DIM]
        o = _softmax_pv([_qk_t(q, k)], [v], _sink_column(sink_ref, kh * ga, ga, t))
        for g in range(ga):
            c = (kh * ga + g) * HEAD_DIM
            oa_ref[:, c:c + HEAD_DIM] = o[g * t:(g + 1) * t].astype(BF16)
    gb = B_HEADS // B_KV
    for kh in range(B_KV):
        q = _stack_heads(z_ref, COL_QB + kh * gb * HEAD_DIM, gb)
        k = z_ref[:, COL_KB + kh * HEAD_DIM:COL_KB + (kh + 1) * HEAD_DIM]
        v = z_ref[:, COL_VB + kh * HEAD_DIM:COL_VB + (kh + 1) * HEAD_DIM]
        o = _softmax_pv([_qk_t(q, k)], [v], None)
        for g in range(gb):
            c = (kh * gb + g) * HEAD_DIM
            ob_ref[:, c:c + HEAD_DIM] = o[g * t:(g + 1) * t].astype(BF16)


def _ctx_attention(z, sink, seq_len):
    n = z.shape[0]
    return pl.pallas_call(
        _ctx_attn_kernel,
        grid=(n // seq_len,),
        in_specs=[pl.BlockSpec(memory_space=pltpu.SMEM),
                  pl.BlockSpec((seq_len, QKV_W), lambda b: (b, 0))],
        out_specs=[pl.BlockSpec((seq_len, A_Q_W), lambda b: (b, 0)),
                   pl.BlockSpec((seq_len, B_Q_W), lambda b: (b, 0))],
        out_shape=[jax.ShapeDtypeStruct((n, A_Q_W), BF16), jax.ShapeDtypeStruct((n, B_Q_W), BF16)],
        compiler_params=_cparams(("arbitrary",), 32),
        name="ctx_attention",
    )(sink, z)


def _win_attn_kernel(sink_ref, q_ref, kp_ref, kc_ref, kn_ref, vp_ref, vc_ref, vn_ref,
                     kx_ref, vx_ref, oa_ref, *, seq_len):
    tq = q_ref.shape[0]
    t0 = pl.program_id(1) * tq
    ga = A_HEADS // A_KV
    nk = tq + 2 * WINDOW
    row = lax.broadcasted_iota(jnp.int32, (ga * tq, nk), 0) & (tq - 1)
    col = lax.broadcasted_iota(jnp.int32, (ga * tq, nk), 1) - WINDOW
    key_pos = col + t0
    valid = (jnp.abs(col - row) <= WINDOW) & (key_pos >= 0) & (key_pos < seq_len)
    for kh in range(A_KV):
        hs = slice(kh * HEAD_DIM, (kh + 1) * HEAD_DIM)
        q = _stack_heads(q_ref, kh * ga * HEAD_DIM, ga)
        kw = jnp.concatenate([kp_ref[:, hs], kc_ref[:, hs], kn_ref[:, hs]], axis=0)
        vw = jnp.concatenate([vp_ref[:, hs], vc_ref[:, hs], vn_ref[:, hs]], axis=0)
        s_win = jnp.where(valid, _qk_t(q, kw), MASK_VALUE)
        s_ctx = _qk_t(q, kx_ref[0, :, hs])
        o = _softmax_pv([s_win, s_ctx], [vw, vx_ref[0, :, hs]],
                        _sink_column(sink_ref, kh * ga, ga, tq))
        for g in range(ga):
            c = (kh * ga + g) * HEAD_DIM
            oa_ref[:, c:c + HEAD_DIM] = o[g * tq:(g + 1) * tq].astype(BF16)


def _win_attention(z, kx, vx, sink, seq_len, tq):
    n = z.shape[0]
    nb = n // seq_len
    tq = min(tq, seq_len)
    assert tq & (tq - 1) == 0 and tq % WINDOW == 0
    nt = seq_len // tq
    halos_per_tile = tq // WINDOW
    n_halo = n // WINDOW
    past = kx.shape[1]
    cur = lambda cb: pl.BlockSpec((tq, A_KV_W), lambda b, t: (b * nt + t, cb))
    prev = lambda cb: pl.BlockSpec(
        (WINDOW, A_KV_W), lambda b, t: (jnp.maximum((b * nt + t) * halos_per_tile - 1, 0), cb))
    nxt = lambda cb: pl.BlockSpec(
        (WINDOW, A_KV_W),
        lambda b, t: (jnp.minimum((b * nt + t + 1) * halos_per_tile, n_halo - 1), cb))
    kcb = COL_KA // A_KV_W
    vcb = COL_VA // A_KV_W
    ctx = pl.BlockSpec((1, past, A_KV_W), lambda b, t: (b, 0, 0))
    return pl.pallas_call(
        functools.partial(_win_attn_kernel, seq_len=seq_len),
        grid=(nb, nt),
        in_specs=[pl.BlockSpec(memory_space=pltpu.SMEM),
                  pl.BlockSpec((tq, A_Q_W), lambda b, t: (b * nt + t, 0)),
                  prev(kcb), cur(kcb), nxt(kcb), prev(vcb), cur(vcb), nxt(vcb), ctx, ctx],
        out_specs=pl.BlockSpec((tq, A_Q_W), lambda b, t: (b * nt + t, 0)),
        out_shape=jax.ShapeDtypeStruct((n, A_Q_W), BF16),
        compiler_params=_cparams(("arbitrary", "arbitrary"), 40),
        name="win_attention",
    )(sink, z, z, z, z, z, z, z, kx, vx)


def _glob_attn_kernel(q_ref, kx_ref, vx_ref, kl_ref, vl_ref, ob_ref, *, tk):
    tq = q_ref.shape[0]
    gb = B_HEADS // B_KV
    q = _stack_heads(q_ref, 0, gb)

    s = _qk_t(q, kx_ref[0])
    m = s.max(axis=-1, keepdims=True)
    p = jnp.exp(s - m)
    l = p.sum(axis=-1, keepdims=True)
    acc = jnp.dot(p.astype(BF16), vx_ref[0], preferred_element_type=F32)
    for c0 in range(0, kl_ref.shape[0], tk):
        s = _qk_t(q, kl_ref[c0:c0 + tk, :])
        m_new = jnp.maximum(m, s.max(axis=-1, keepdims=True))
        alpha = jnp.exp(m - m_new)
        p = jnp.exp(s - m_new)
        l = alpha * l + p.sum(axis=-1, keepdims=True)
        acc = alpha * acc + jnp.dot(p.astype(BF16), vl_ref[c0:c0 + tk, :],
                                    preferred_element_type=F32)
        m = m_new
    o = acc * (1.0 / l)
    for g in range(gb):
        ob_ref[:, g * HEAD_DIM:(g + 1) * HEAD_DIM] = o[g * tq:(g + 1) * tq].astype(BF16)


def _glob_attention(z, kx, vx, seq_len, tq, tk):
    n = z.shape[0]
    nb = n // seq_len
    tq = min(tq, seq_len)
    tk = min(tk, seq_len)
    nt = seq_len // tq
    past = kx.shape[1]
    q_w = B_Q_W // B_KV
    ctx = pl.BlockSpec((1, past, HEAD_DIM), lambda b, kh, t: (b, 0, kh))
    return pl.pallas_call(
        functools.partial(_glob_attn_kernel, tk=tk),
        grid=(nb, B_KV, nt),
        in_specs=[pl.BlockSpec((tq, q_w), lambda b, kh, t: (b * nt + t, COL_QB // q_w + kh)),
                  ctx, ctx,
                  pl.BlockSpec((seq_len, HEAD_DIM), lambda b, kh, t: (b, COL_KB // HEAD_DIM + kh)),
                  pl.BlockSpec((seq_len, HEAD_DIM), lambda b, kh, t: (b, COL_VB // HEAD_DIM + kh))],
        out_specs=pl.BlockSpec((tq, q_w), lambda b, kh, t: (b * nt + t, kh)),
        out_shape=jax.ShapeDtypeStruct((n, B_Q_W), BF16),
        compiler_params=_cparams(("arbitrary", "arbitrary", "arbitrary"), 48),
        name="glob_attention",
    )(z, kx, vx, z, z)


def _conv_kernel(cp_ref, cc_ref, cn_ref, w_ref, b_ref, g_ref, beta_ref, o_ref, u_ref, *,
                 tiles_per_seq):
    tc = cc_ref.shape[0]
    ti = pl.program_id(0) % tiles_per_seq

    def glu(ref):
        return ref[:, :C_CH] * jax.nn.sigmoid(ref[:, C_CH:])

    u_ref[0:CONV_HALO, :] = jnp.where(ti > 0, glu(cp_ref), 0.0)
    u_ref[CONV_HALO:CONV_HALO + tc, :] = glu(cc_ref)
    u_ref[CONV_HALO + tc:, :] = jnp.where(ti < tiles_per_seq - 1, glu(cn_ref), 0.0)

    bias = b_ref[...]
    gamma = g_ref[...]
    beta = beta_ref[...]
    base = CONV_HALO - CONV_W // 2

    def body(r, carry):
        r0 = pl.multiple_of(r * CONV_ROWS, CONV_ROWS)
        y = None
        for b in range(SUBLANES):
            v = None
            for a in range((base + CONV_W - 1 - b) // SUBLANES + 1):
                k = SUBLANES * a + b - base
                if k < 0:
                    continue
                term = w_ref[k:k + 1, :] * u_ref[pl.ds(r0 + SUBLANES * a, CONV_ROWS + SUBLANES), :]
                v = term if v is None else v + term
            v = v[b:b + CONV_ROWS, :]
            y = v if y is None else y + v
        y = y + bias
        mu = jnp.mean(y, axis=-1, keepdims=True)
        yc = y - mu
        yn = yc * lax.rsqrt(jnp.mean(yc * yc, axis=-1, keepdims=True) + EPS) * gamma + beta
        o_ref[pl.ds(r0, CONV_ROWS), :] = jax.nn.silu(yn).astype(BF16)
        return carry

    lax.fori_loop(0, tc // CONV_ROWS, body, 0, unroll=CONV_UNROLL)


def _conformer_conv(cz, conv_w, conv_b, ln_g, ln_b, seq_len, tc):
    n = cz.shape[0]
    tc = min(tc, seq_len)
    tiles_per_seq = seq_len // tc
    halos_per_tile = tc // CONV_HALO
    n_halo = n // CONV_HALO
    const = lambda i: (0, 0)
    return pl.pallas_call(
        functools.partial(_conv_kernel, tiles_per_seq=tiles_per_seq),
        grid=(n // tc,),
        in_specs=[pl.BlockSpec((CONV_HALO, CZ_W),
                               lambda i: (jnp.maximum(i * halos_per_tile - 1, 0), 0)),
                  pl.BlockSpec((tc, CZ_W), lambda i: (i, 0)),
                  pl.BlockSpec((CONV_HALO, CZ_W),
                               lambda i: (jnp.minimum((i + 1) * halos_per_tile, n_halo - 1), 0)),
                  pl.BlockSpec((CONV_W, C_CH), const),
                  pl.BlockSpec((1, C_CH), const),
                  pl.BlockSpec((1, C_CH), const),
                  pl.BlockSpec((1, C_CH), const)],
        out_specs=pl.BlockSpec((tc, C_CH), lambda i: (i, 0)),
        out_shape=jax.ShapeDtypeStruct((n, C_CH), BF16),
        scratch_shapes=[pltpu.VMEM((tc + 2 * CONV_HALO, C_CH), F32)],
        compiler_params=_cparams(("arbitrary",), 32),
        name="conformer_conv",
    )(cz, cz, cz, conv_w, conv_b, ln_g, ln_b)


def _outproj_kernel(oa_ref, ob_ref, oc_ref, x_ref, mod_ref, g2_ref, w_ref, xo_ref, h2_ref):
    y = jnp.dot(oa_ref[...], w_ref[0:A_Q_W, :], preferred_element_type=F32)
    y = y + jnp.dot(ob_ref[...], w_ref[A_Q_W:A_Q_W + B_Q_W, :], preferred_element_type=F32)
    y = y + jnp.dot(oc_ref[...], w_ref[A_Q_W + B_Q_W:, :], preferred_element_type=F32)
    x = x_ref[...] + mod_ref[0, 2:3, :] * y
    xo_ref[...] = x
    h2 = _rms(x, g2_ref[...])
    h2_ref[...] = (h2 * (1.0 + mod_ref[0, 4:5, :]) + mod_ref[0, 3:4, :]).astype(BF16)


def _out_projection(oa, ob, oc, x, mod, g2, w_out, seq_len, tm):
    n, d = x.shape
    tm, tiles_per_seq = _token_tiling(n, seq_len, mod.shape[0], tm)
    mod_map = lambda i: (i // tiles_per_seq, 0, 0)
    const = lambda i: (0, 0)
    row = lambda w: pl.BlockSpec((tm, w), lambda i: (i, 0))
    return pl.pallas_call(
        _outproj_kernel,
        grid=(n // tm,),
        in_specs=[row(A_Q_W), row(B_Q_W), row(C_CH), row(d),
                  pl.BlockSpec((1, N_MOD, d), mod_map),
                  pl.BlockSpec((1, d), const),
                  pl.BlockSpec(w_out.shape, const, pipeline_mode=pl.Buffered(1))],
        out_specs=[row(d), row(d)],
        out_shape=[jax.ShapeDtypeStruct((n, d), F32), jax.ShapeDtypeStruct((n, d), BF16)],
        compiler_params=_cparams(("arbitrary",), 56),
        name="out_projection",
    )(oa, ob, oc, x, mod, g2, w_out)


def _ffn_kernel(h_ref, wg_ref, wu_ref, wd_ref, x_ref, mod_ref, o_ref, acc_ref):
    f = pl.program_id(1)
    h = h_ref[...]
    g = jnp.dot(h, wg_ref[...], preferred_element_type=F32)
    u = jnp.dot(h, wu_ref[...], preferred_element_type=F32)
    a = (jax.nn.silu(g) * u).astype(BF16)
    last = pl.num_programs(1) - 1

    @pl.when(f == 0)
    def _():
        acc_ref[...] = jnp.dot(a, wd_ref[...], preferred_element_type=F32)

    @pl.when((f > 0) & (f < last))
    def _():
        acc_ref[...] += jnp.dot(a, wd_ref[...], preferred_element_type=F32)

    @pl.when(f == last)
    def _():
        acc = acc_ref[...] + jnp.dot(a, wd_ref[...], preferred_element_type=F32)
        o_ref[...] = x_ref[...] + mod_ref[0, 5:6, :] * acc


def _ffn(h2, w_gate, w_up, w_down, x, mod, seq_len, tm, tf):
    n, d = x.shape
    d_ff = w_gate.shape[1]
    assert d_ff // tf >= 2
    tm, tiles_per_seq = _token_tiling(n, seq_len, mod.shape[0], tm)
    mod_map = lambda i, f: (i // tiles_per_seq, 0, 0)
    return pl.pallas_call(
        _ffn_kernel,
        grid=(n // tm, d_ff // tf),
        in_specs=[pl.BlockSpec((tm, d), lambda i, f: (i, 0)),
                  pl.BlockSpec((d, tf), lambda i, f: (0, f)),
                  pl.BlockSpec((d, tf), lambda i, f: (0, f)),
                  pl.BlockSpec((tf, d), lambda i, f: (f, 0)),
                  pl.BlockSpec((tm, d), lambda i, f: (i, 0)),
                  pl.BlockSpec((1, N_MOD, d), mod_map)],
        out_specs=pl.BlockSpec((tm, d), lambda i, f: (i, 0)),
        out_shape=jax.ShapeDtypeStruct((n, d), F32),
        scratch_shapes=[pltpu.VMEM((tm, d), F32)],
        compiler_params=_cparams(("arbitrary", "arbitrary"), 56),
        name="swiglu_ffn",
    )(h2, w_gate, w_up, w_down, x, mod)


def _rope_tables(n_tokens):
    pairs = HEAD_DIM // 4
    t = jnp.arange(n_tokens)
    row = (t // GRID_W).astype(F32)
    col = (t % GRID_W).astype(F32)
    inv = jnp.power(ROPE_BASE, -jnp.arange(pairs, dtype=F32) / pairs)
    ang = jnp.concatenate([row[:, None] * inv, col[:, None] * inv], axis=-1)
    cos = jnp.cos(ang)
    sin = jnp.sin(ang)
    return jnp.concatenate([cos, cos], axis=-1), jnp.concatenate([-sin, sin], axis=-1)


def _trunk_layer(x, mod, lw, seq_len, rope_tabs=None, ctx_kv=None):
    (w_in, w_out, w_gate, w_up, w_down, g1, g2, gains, sink, conv_w, conv_b, cln_g, cln_b) = lw
    latent = ctx_kv is not None
    proj = _in_projection(x, mod, g1, w_in, gains, rope_tabs, seq_len, emit_kv=not latent, tm=512)
    z, cz = proj[0], proj[1]
    if latent:
        ka_c, va_c, kb_c, vb_c = ctx_kv
        oa = _win_attention(z, ka_c, va_c, sink, seq_len, tq=512)
        ob = _glob_attention(z, kb_c, vb_c, seq_len, tq=256, tk=1024)
    else:
        oa, ob = _ctx_attention(z, sink, seq_len)
    oc = _conformer_conv(cz, conv_w, conv_b, cln_g, cln_b, seq_len, tc=512)
    x, h2 = _out_projection(oa, ob, oc, x, mod, g2, w_out, seq_len, tm=512)
    x = _ffn(h2, w_gate, w_up, w_down, x, mod, seq_len, tm=512, tf=512)
    return x, proj[2:]


def kernel(x_prompt, x_sample, cache_a_k, cache_a_v, cache_b_k, cache_b_v, c, c_ctx, w_ada, b_ada, w_in, w_out, w_gate, w_up, w_down, norm1_g, norm2_g, qnorm_a_g, knorm_a_g, qnorm_b_g, knorm_b_g, sink_a, conv_w, conv_b, conv_ln_g, conv_ln_b):
    batch, seq, d = x_prompt.shape
    dec_batch, dec_seq, _ = x_sample.shape
    depth = w_in.shape[0]
    past = cache_a_k.shape[2]

    n_rows = 1 + dec_batch
    pad_rows = -n_rows % 8
    cvecs = jnp.concatenate([c_ctx[None, :], c, jnp.zeros((pad_rows, d), F32)], axis=0)
    mods = _ada_mods(cvecs, w_ada, b_ada).reshape(depth, n_rows + pad_rows, N_MOD, d)

    rope_tabs = _rope_tables(dec_seq)
    w_in_b, w_out_b = w_in.astype(BF16), w_out.astype(BF16)
    w_gate_b, w_up_b, w_down_b = w_gate.astype(BF16), w_up.astype(BF16), w_down.astype(BF16)
    ctx_kv = [t.astype(BF16).reshape(dec_batch, depth, past, -1)
              for t in (cache_a_k, cache_a_v, cache_b_k, cache_b_v)]

    y_prompt = x_prompt.reshape(batch * seq, d)
    y_sample = x_sample.reshape(dec_batch * dec_seq, d)
    new_kv = []
    for l in range(depth):
        gains = jnp.stack([qnorm_a_g[l], knorm_a_g[l], qnorm_b_g[l], knorm_b_g[l]])
        lw = (w_in_b[l], w_out_b[l], w_gate_b[l], w_up_b[l], w_down_b[l],
              norm1_g[l][None, :], norm2_g[l][None, :], gains, sink_a[l],
              conv_w[l], conv_b[l][None, :], conv_ln_g[l][None, :], conv_ln_b[l][None, :])
        y_prompt, kv = _trunk_layer(y_prompt, mods[l, 0:1], lw, seq)
        new_kv.append(kv)
        y_sample, _ = _trunk_layer(y_sample, mods[l, 1:n_rows], lw, dec_seq, rope_tabs,
                                   [t[:, l] for t in ctx_kv])

    def stacked(i, n_kv):
        return jnp.stack([kv[i].reshape(batch, seq, n_kv, HEAD_DIM) for kv in new_kv], axis=1)

    return (y_prompt.reshape(batch, seq, d), y_sample.reshape(dec_batch, dec_seq, d),
            stacked(0, A_KV), stacked(1, A_KV), stacked(2, B_KV), stacked(3, B_KV))
```

```python
import functools

import jax
import jax.numpy as jnp
from jax import lax
from jax.experimental import pallas as pl
from jax.experimental.pallas import tpu as pltpu

F32 = jnp.float32
BF16 = jnp.bfloat16

HEAD_DIM = 128
GRID_W = 64
A_HEADS = 4
A_KV = 2
B_HEADS = 8
B_KV = 2
C_CH = 512
CONV_W = 31
WINDOW = 128
ROPE_BASE = 10000.0
N_MOD = 6
EPS = 1e-6

A_Q_W = A_HEADS * HEAD_DIM
A_KV_W = A_KV * HEAD_DIM
B_Q_W = B_HEADS * HEAD_DIM
B_KV_W = B_KV * HEAD_DIM
QKV_W = A_Q_W + 2 * A_KV_W + B_Q_W + 2 * B_KV_W
CZ_W = 2 * C_CH
COL_KA = A_Q_W
COL_VA = COL_KA + A_KV_W
COL_QB = COL_VA + A_KV_W
COL_KB = COL_QB + B_Q_W
COL_VB = COL_KB + B_KV_W

V7X_VMEM_BYTES = 64 * 1024 * 1024
MASK_VALUE = -1e30
LOG2_E = 1.4426950408889634
CONV_HALO = 16
CONV_ROWS = 32
CONV_UNROLL = 4
ATTN_LOOKAHEAD = 14
SUBLANES = 8
BF16_ROWS = 16
VT_ROWS = HEAD_DIM + BF16_ROWS


def _cparams(semantics, vmem_mb):
    assert vmem_mb * 1024 * 1024 < V7X_VMEM_BYTES
    return pltpu.CompilerParams(dimension_semantics=semantics,
                                vmem_limit_bytes=vmem_mb * 1024 * 1024)


def _rms(x, g):
    return x * lax.rsqrt(jnp.mean(x * x, axis=-1, keepdims=True) + EPS) * g


def _qk_t(q, k):
    return lax.dot_general(q, k, (((1,), (1,)), ((), ())), preferred_element_type=F32)


def _token_tiling(n_tokens, seq_len, n_mod_rows, tm):
    span = n_tokens if n_mod_rows == 1 else seq_len
    tm = min(tm, span)
    assert span % tm == 0
    return tm, span // tm


def _mods_kernel(c_ref, w_ref, b_ref, o_ref):
    a = jax.nn.silu(c_ref[...]).astype(BF16)
    o_ref[0] = jnp.dot(a, w_ref[0].astype(BF16), preferred_element_type=F32) + b_ref[0]


def _ada_mods(cvecs, w_ada, b_ada):
    depth, d, n = w_ada.shape
    r = cvecs.shape[0]
    tn = 1024
    return pl.pallas_call(
        _mods_kernel,
        grid=(depth, n // tn),
        in_specs=[pl.BlockSpec((r, d), lambda l, j: (0, 0)),
                  pl.BlockSpec((1, d, tn), lambda l, j: (l, 0, j)),
                  pl.BlockSpec((1, 1, tn), lambda l, j: (l, 0, j))],
        out_specs=pl.BlockSpec((1, r, tn), lambda l, j: (l, 0, j)),
        out_shape=jax.ShapeDtypeStruct((depth, r, n), F32),
        compiler_params=_cparams(("arbitrary", "arbitrary"), 40),
        name="ada_mods",
    )(cvecs, w_ada, b_ada.reshape(depth, 1, n))


def _inproj_kernel(*refs, rope, emit_kv, emit_vt):
    x_ref, mod_ref, g1_ref, w_ref, gains_ref = refs[:5]
    refs = refs[5:]
    if rope:
        cos_ref, sin_ref = refs[:2]
        refs = refs[2:]
    z_ref, cz_ref = refs[:2]
    refs = refs[2:]
    if emit_kv:
        ka_ref, va_ref, kb_ref, vb_ref = refs[:4]
        refs = refs[4:]
    if emit_vt:
        vt_ref, = refs

    h = _rms(x_ref[...], g1_ref[...])
    h = h * (1.0 + mod_ref[0, 1:2, :]) + mod_ref[0, 0:1, :]
    hb = h.astype(BF16)
    if rope:
        cos = cos_ref[...]
        sin = sin_ref[...]
    q_scale = HEAD_DIM ** -0.5
    q_scale_b = q_scale * LOG2_E

    def normed(zh, gain_idx):
        return _rms(zh, gains_ref[gain_idx:gain_idx + 1, :])

    def rotated(zh):
        if not rope:
            return zh
        return zh * cos + pltpu.roll(zh, HEAD_DIM // 2, axis=1) * sin

    chunk = 512
    for c0 in range(0, QKV_W, chunk):
        zc = jnp.dot(hb, w_ref[:, c0:c0 + chunk], preferred_element_type=F32)
        for j in range(chunk // HEAD_DIM):
            col = c0 + j * HEAD_DIM
            zh = zc[:, j * HEAD_DIM:(j + 1) * HEAD_DIM]
            if col < COL_KA:
                out = rotated(normed(zh, 0)) * q_scale
            elif col < COL_VA:
                kn = normed(zh, 1)
                if emit_kv:
                    ka_ref[:, col - COL_KA:col - COL_KA + HEAD_DIM] = kn
                out = rotated(kn)
            elif col < COL_QB:
                if emit_kv:
                    va_ref[:, col - COL_VA:col - COL_VA + HEAD_DIM] = zh
                out = zh
            elif col < COL_KB:
                out = rotated(normed(zh, 2)) * q_scale_b
            elif col < COL_VB:
                kn = normed(zh, 3)
                if emit_kv:
                    kb_ref[:, col - COL_KB:col - COL_KB + HEAD_DIM] = kn
                out = rotated(kn)
            else:
                if emit_kv:
                    vb_ref[:, col - COL_VB:col - COL_VB + HEAD_DIM] = zh
                if emit_vt:
                    r0 = (col - COL_VB) // HEAD_DIM * VT_ROWS
                    vt_ref[r0:r0 + HEAD_DIM, :] = zh.T.astype(BF16)
                    vt_ref[r0 + HEAD_DIM:r0 + VT_ROWS, :] = jnp.ones((BF16_ROWS, zh.shape[0]), BF16)
                out = zh
            z_ref[:, col:col + HEAD_DIM] = out.astype(BF16)
    for c0 in range(0, CZ_W, chunk):
        cz_ref[:, c0:c0 + chunk] = jnp.dot(hb, w_ref[:, QKV_W + c0:QKV_W + c0 + chunk],
                                           preferred_element_type=F32)


def _in_projection(x, mod, g1, w_in, gains, rope_tabs, seq_len, tm):
    n, d = x.shape
    tm, tiles_per_seq = _token_tiling(n, seq_len, mod.shape[0], tm)
    mod_map = lambda i: (i // tiles_per_seq, 0, 0)
    const = lambda i: (0, 0)
    in_specs = [pl.BlockSpec((tm, d), lambda i: (i, 0)),
                pl.BlockSpec((1, N_MOD, d), mod_map),
                pl.BlockSpec((1, d), const),
                pl.BlockSpec(w_in.shape, const, pipeline_mode=pl.Buffered(1)),
                pl.BlockSpec(gains.shape, const)]
    args = [x, mod, g1, w_in, gains]
    rope = rope_tabs is not None
    emit_kv, emit_vt = not rope, rope
    if rope:
        in_specs += [pl.BlockSpec((tm, HEAD_DIM), lambda i: (i % tiles_per_seq, 0))] * 2
        args += list(rope_tabs)
    out_specs = [pl.BlockSpec((tm, QKV_W), lambda i: (i, 0)),
                 pl.BlockSpec((tm, CZ_W), lambda i: (i, 0))]
    out_shape = [jax.ShapeDtypeStruct((n, QKV_W), BF16), jax.ShapeDtypeStruct((n, CZ_W), F32)]
    if emit_kv:
        out_specs += [pl.BlockSpec((tm, A_KV_W), lambda i: (i, 0))] * 4
        out_shape += [jax.ShapeDtypeStruct((n, A_KV_W), F32)] * 4
    if emit_vt:
        out_specs += [pl.BlockSpec((B_KV * VT_ROWS, tm), lambda i: (0, i))]
        out_shape += [jax.ShapeDtypeStruct((B_KV * VT_ROWS, n), BF16)]
    return pl.pallas_call(
        functools.partial(_inproj_kernel, rope=rope, emit_kv=emit_kv, emit_vt=emit_vt),
        grid=(n // tm,),
        in_specs=in_specs,
        out_specs=out_specs,
        out_shape=out_shape,
        compiler_params=_cparams(("arbitrary",), 56),
        name="in_projection_lat" if rope else "in_projection_ctx",
    )(*args)


def _softmax_pv(s_list, v_list, sink_col, exp=jnp.exp):
    m = s_list[0].max(axis=-1, keepdims=True)
    for s in s_list[1:]:
        m = jnp.maximum(m, s.max(axis=-1, keepdims=True))
    if sink_col is not None:
        m = jnp.maximum(m, sink_col)
    denom = None
    acc = None
    for s, v in zip(s_list, v_list):
        p = exp(s - m)
        ps = p.sum(axis=-1, keepdims=True)
        pv = jnp.dot(p.astype(BF16), v, preferred_element_type=F32)
        denom = ps if denom is None else denom + ps
        acc = pv if acc is None else acc + pv
    if sink_col is not None:
        denom = denom + exp(sink_col - m)
    return acc * (1.0 / denom)


def _stack_heads(ref, col0, n_heads):
    return jnp.concatenate([ref[:, col0 + g * HEAD_DIM:col0 + (g + 1) * HEAD_DIM]
                            for g in range(n_heads)], axis=0)


def _sink_column(sink_ref, head0, n_heads, rows_per_head):
    row = lax.broadcasted_iota(jnp.int32, (n_heads * rows_per_head, 1), 0)
    col = jnp.full((n_heads * rows_per_head, 1), sink_ref[head0], F32)
    for g in range(1, n_heads):
        col = jnp.where(row >= g * rows_per_head, sink_ref[head0 + g], col)
    return col


def _ctx_attn_kernel(sink_ref, z_ref, oa_ref, ob_ref):
    t = z_ref.shape[0]
    ga = A_HEADS // A_KV
    for kh in range(A_KV):
        q = _stack_heads(z_ref, kh * ga * HEAD_DIM, ga)
        k = z_ref[:, COL_KA + kh * HEAD_DIM:COL_KA + (kh + 1) * HEAD_DIM]
        v = z_ref[:, COL_VA + kh * HEAD_DIM:COL_VA + (kh + 1) * HEAD_DIM]
        o = _softmax_pv([_qk_t(q, k)], [v], _sink_column(sink_ref, kh * ga, ga, t))
        for g in range(ga):
            c = (kh * ga + g) * HEAD_DIM
            oa_ref[:, c:c + HEAD_DIM] = o[g * t:(g + 1) * t].astype(BF16)
    gb = B_HEADS // B_KV
    for kh in range(B_KV):
        q = _stack_heads(z_ref, COL_QB + kh * gb * HEAD_DIM, gb)
        k = z_ref[:, COL_KB + kh * HEAD_DIM:COL_KB + (kh + 1) * HEAD_DIM]
        v = z_ref[:, COL_VB + kh * HEAD_DIM:COL_VB + (kh + 1) * HEAD_DIM]
        o = _softmax_pv([_qk_t(q, k)], [v], None, exp=jnp.exp2)
        for g in range(gb):
            c = (kh * gb + g) * HEAD_DIM
            ob_ref[:, c:c + HEAD_DIM] = o[g * t:(g + 1) * t].astype(BF16)


def _ctx_attention(z, sink, seq_len):
    n = z.shape[0]
    return pl.pallas_call(
        _ctx_attn_kernel,
        grid=(n // seq_len,),
        in_specs=[pl.BlockSpec(memory_space=pltpu.SMEM),
                  pl.BlockSpec((seq_len, QKV_W), lambda b: (b, 0))],
        out_specs=[pl.BlockSpec((seq_len, A_Q_W), lambda b: (b, 0)),
                   pl.BlockSpec((seq_len, B_Q_W), lambda b: (b, 0))],
        out_shape=[jax.ShapeDtypeStruct((n, A_Q_W), BF16), jax.ShapeDtypeStruct((n, B_Q_W), BF16)],
        compiler_params=_cparams(("arbitrary",), 32),
        name="ctx_attention",
    )(sink, z)


def _win_attn_kernel(sink_ref, q_ref, kp_ref, kc_ref, kn_ref, vp_ref, vc_ref, vn_ref,
                     kx_ref, vx_ref, oa_ref, *, seq_len):
    tq = q_ref.shape[0]
    t0 = pl.program_id(1) * tq
    ga = A_HEADS // A_KV
    nk = tq + 2 * WINDOW
    row = lax.broadcasted_iota(jnp.int32, (ga * tq, nk), 0) & (tq - 1)
    col = lax.broadcasted_iota(jnp.int32, (ga * tq, nk), 1) - WINDOW
    key_pos = col + t0
    valid = (jnp.abs(col - row) <= WINDOW) & (key_pos >= 0) & (key_pos < seq_len)
    for kh in range(A_KV):
        hs = slice(kh * HEAD_DIM, (kh + 1) * HEAD_DIM)
        q = _stack_heads(q_ref, kh * ga * HEAD_DIM, ga)
        kw = jnp.concatenate([kp_ref[:, hs], kc_ref[:, hs], kn_ref[:, hs]], axis=0)
        vw = jnp.concatenate([vp_ref[:, hs], vc_ref[:, hs], vn_ref[:, hs]], axis=0)
        s_win = jnp.where(valid, _qk_t(q, kw), MASK_VALUE)
        s_ctx = _qk_t(q, kx_ref[0, :, hs])
        o = _softmax_pv([s_win, s_ctx], [vw, vx_ref[0, :, hs]],
                        _sink_column(sink_ref, kh * ga, ga, tq))
        for g in range(ga):
            c = (kh * ga + g) * HEAD_DIM
            oa_ref[:, c:c + HEAD_DIM] = o[g * tq:(g + 1) * tq].astype(BF16)


def _win_attention(z, kx, vx, sink, seq_len, tq):
    n = z.shape[0]
    nb = n // seq_len
    tq = min(tq, seq_len)
    assert tq & (tq - 1) == 0 and tq % WINDOW == 0
    nt = seq_len // tq
    halos_per_tile = tq // WINDOW
    n_halo = n // WINDOW
    past = kx.shape[1]
    cur = lambda cb: pl.BlockSpec((tq, A_KV_W), lambda b, t: (b * nt + t, cb))
    prev = lambda cb: pl.BlockSpec(
        (WINDOW, A_KV_W), lambda b, t: (jnp.maximum((b * nt + t) * halos_per_tile - 1, 0), cb))
    nxt = lambda cb: pl.BlockSpec(
        (WINDOW, A_KV_W),
        lambda b, t: (jnp.minimum((b * nt + t + 1) * halos_per_tile, n_halo - 1), cb))
    kcb = COL_KA // A_KV_W
    vcb = COL_VA // A_KV_W
    ctx = pl.BlockSpec((1, past, A_KV_W), lambda b, t: (b, 0, 0))
    return pl.pallas_call(
        functools.partial(_win_attn_kernel, seq_len=seq_len),
        grid=(nb, nt),
        in_specs=[pl.BlockSpec(memory_space=pltpu.SMEM),
                  pl.BlockSpec((tq, A_Q_W), lambda b, t: (b * nt + t, 0)),
                  prev(kcb), cur(kcb), nxt(kcb), prev(vcb), cur(vcb), nxt(vcb), ctx, ctx],
        out_specs=pl.BlockSpec((tq, A_Q_W), lambda b, t: (b * nt + t, 0)),
        out_shape=jax.ShapeDtypeStruct((n, A_Q_W), BF16),
        compiler_params=_cparams(("arbitrary", "arbitrary"), 40),
        name="win_attention",
    )(sink, z, z, z, z, z, z, z, kx, vx)


def _glob_attn_kernel(q_ref, kx_ref, vxt_ref, kl_ref, vlt_ref, ob_ref, *, tk):
    gb = B_HEADS // B_KV
    past = kx_ref.shape[1]
    assert past % tk == 0 and kl_ref.shape[0] % tk == 0
    heads = [q_ref[:, g * HEAD_DIM:(g + 1) * HEAD_DIM] for g in range(gb)]

    def update(st, vt, carry):
        c_max = st.max(axis=0, keepdims=True)
        if carry is None:
            m = c_max
            pt = jnp.exp2(st - m)
            return m, jnp.dot(vt, pt.astype(BF16), preferred_element_type=F32)
        m_old, acc = carry
        m = jnp.maximum(m_old, c_max)
        alpha = jnp.exp2(m_old - m)
        pt = jnp.exp2(st - m)
        acc = alpha * acc + jnp.dot(vt, pt.astype(BF16), preferred_element_type=F32)
        return m, acc

    def keys(i):
        c0 = i * tk
        if c0 < past:
            return kx_ref[0, c0:c0 + tk, :], vxt_ref[0, :, c0:c0 + tk]
        c0 -= past
        return kl_ref[c0:c0 + tk, :], vlt_ref[:, c0:c0 + tk]

    n_tiles = (past + kl_ref.shape[0]) // tk * gb
    score = lambda t: _qk_t(keys(t // gb)[0], heads[t % gb])
    carries = [None] * gb
    pending = [score(t) for t in range(min(ATTN_LOOKAHEAD, n_tiles))]
    for t in range(n_tiles):
        if t + ATTN_LOOKAHEAD < n_tiles:
            pending.append(score(t + ATTN_LOOKAHEAD))
        carries[t % gb] = update(pending.pop(0), keys(t // gb)[1], carries[t % gb])
    for g, (_, acc) in enumerate(carries):
        o = acc[:HEAD_DIM] * (1.0 / acc[HEAD_DIM:HEAD_DIM + 1])
        ob_ref[:, g * HEAD_DIM:(g + 1) * HEAD_DIM] = o.T.astype(BF16)


def _glob_attention(z, vt, kx, vxt, seq_len, tq, tk):
    n = z.shape[0]
    nb = n // seq_len
    tq = min(tq, seq_len)
    tk = min(tk, seq_len)
    nt = seq_len // tq
    past = kx.shape[1]
    q_w = B_Q_W // B_KV
    return pl.pallas_call(
        functools.partial(_glob_attn_kernel, tk=tk),
        grid=(nb, B_KV, nt),
        in_specs=[pl.BlockSpec((tq, q_w), lambda b, kh, t: (b * nt + t, COL_QB // q_w + kh)),
                  pl.BlockSpec((1, past, HEAD_DIM), lambda b, kh, t: (b, 0, kh)),
                  pl.BlockSpec((1, VT_ROWS, past), lambda b, kh, t: (b, kh, 0)),
                  pl.BlockSpec((seq_len, HEAD_DIM), lambda b, kh, t: (b, COL_KB // HEAD_DIM + kh)),
                  pl.BlockSpec((VT_ROWS, seq_len), lambda b, kh, t: (kh, b))],
        out_specs=pl.BlockSpec((tq, q_w), lambda b, kh, t: (b * nt + t, kh)),
        out_shape=jax.ShapeDtypeStruct((n, B_Q_W), BF16),
        compiler_params=_cparams(("arbitrary", "arbitrary", "arbitrary"), 48),
        name="glob_attention",
    )(z, kx, vxt, z, vt)


def _conv_kernel(cp_ref, cc_ref, cn_ref, w_ref, b_ref, g_ref, beta_ref, o_ref, u_ref, *,
                 tiles_per_seq):
    tc = cc_ref.shape[0]
    ti = pl.program_id(0) % tiles_per_seq

    def glu(ref):
        return ref[:, :C_CH] * jax.nn.sigmoid(ref[:, C_CH:])

    u_ref[0:CONV_HALO, :] = jnp.where(ti > 0, glu(cp_ref), 0.0)
    u_ref[CONV_HALO:CONV_HALO + tc, :] = glu(cc_ref)
    u_ref[CONV_HALO + tc:, :] = jnp.where(ti < tiles_per_seq - 1, glu(cn_ref), 0.0)

    bias = b_ref[...]
    gamma = g_ref[...]
    beta = beta_ref[...]
    base = CONV_HALO - CONV_W // 2

    def body(r, carry):
        r0 = pl.multiple_of(r * CONV_ROWS, CONV_ROWS)
        y = None
        for b in range(SUBLANES):
            v = None
            for a in range((base + CONV_W - 1 - b) // SUBLANES + 1):
                k = SUBLANES * a + b - base
                if k < 0:
                    continue
                term = w_ref[k:k + 1, :] * u_ref[pl.ds(r0 + SUBLANES * a, CONV_ROWS + SUBLANES), :]
                v = term if v is None else v + term
            v = v[b:b + CONV_ROWS, :]
            y = v if y is None else y + v
        y = y + bias
        mu = jnp.mean(y, axis=-1, keepdims=True)
        yc = y - mu
        yn = yc * lax.rsqrt(jnp.mean(yc * yc, axis=-1, keepdims=True) + EPS) * gamma + beta
        o_ref[pl.ds(r0, CONV_ROWS), :] = jax.nn.silu(yn).astype(BF16)
        return carry

    lax.fori_loop(0, tc // CONV_ROWS, body, 0, unroll=CONV_UNROLL)


def _conformer_conv(cz, conv_w, conv_b, ln_g, ln_b, seq_len, tc):
    n = cz.shape[0]
    tc = min(tc, seq_len)
    tiles_per_seq = seq_len // tc
    halos_per_tile = tc // CONV_HALO
    n_halo = n // CONV_HALO
    const = lambda i: (0, 0)
    return pl.pallas_call(
        functools.partial(_conv_kernel, tiles_per_seq=tiles_per_seq),
        grid=(n // tc,),
        in_specs=[pl.BlockSpec((CONV_HALO, CZ_W),
                               lambda i: (jnp.maximum(i * halos_per_tile - 1, 0), 0)),
                  pl.BlockSpec((tc, CZ_W), lambda i: (i, 0)),
                  pl.BlockSpec((CONV_HALO, CZ_W),
                               lambda i: (jnp.minimum((i + 1) * halos_per_tile, n_halo - 1), 0)),
                  pl.BlockSpec((CONV_W, C_CH), const),
                  pl.BlockSpec((1, C_CH), const),
                  pl.BlockSpec((1, C_CH), const),
                  pl.BlockSpec((1, C_CH), const)],
        out_specs=pl.BlockSpec((tc, C_CH), lambda i: (i, 0)),
        out_shape=jax.ShapeDtypeStruct((n, C_CH), BF16),
        scratch_shapes=[pltpu.VMEM((tc + 2 * CONV_HALO, C_CH), F32)],
        compiler_params=_cparams(("arbitrary",), 32),
        name="conformer_conv",
    )(cz, cz, cz, conv_w, conv_b, ln_g, ln_b)


def _outproj_kernel(oa_ref, ob_ref, oc_ref, x_ref, mod_ref, g2_ref, w_ref, xo_ref, h2_ref):
    y = jnp.dot(oa_ref[...], w_ref[0:A_Q_W, :], preferred_element_type=F32)
    y = y + jnp.dot(ob_ref[...], w_ref[A_Q_W:A_Q_W + B_Q_W, :], preferred_element_type=F32)
    y = y + jnp.dot(oc_ref[...], w_ref[A_Q_W + B_Q_W:, :], preferred_element_type=F32)
    x = x_ref[...] + mod_ref[0, 2:3, :] * y
    xo_ref[...] = x
    h2 = _rms(x, g2_ref[...])
    h2_ref[...] = (h2 * (1.0 + mod_ref[0, 4:5, :]) + mod_ref[0, 3:4, :]).astype(BF16)


def _out_projection(oa, ob, oc, x, mod, g2, w_out, seq_len, tm):
    n, d = x.shape
    tm, tiles_per_seq = _token_tiling(n, seq_len, mod.shape[0], tm)
    mod_map = lambda i: (i // tiles_per_seq, 0, 0)
    const = lambda i: (0, 0)
    row = lambda w: pl.BlockSpec((tm, w), lambda i: (i, 0))
    return pl.pallas_call(
        _outproj_kernel,
        grid=(n // tm,),
        in_specs=[row(A_Q_W), row(B_Q_W), row(C_CH), row(d),
                  pl.BlockSpec((1, N_MOD, d), mod_map),
                  pl.BlockSpec((1, d), const),
                  pl.BlockSpec(w_out.shape, const, pipeline_mode=pl.Buffered(1))],
        out_specs=[row(d), row(d)],
        out_shape=[jax.ShapeDtypeStruct((n, d), F32), jax.ShapeDtypeStruct((n, d), BF16)],
        compiler_params=_cparams(("arbitrary",), 56),
        name="out_projection",
    )(oa, ob, oc, x, mod, g2, w_out)


def _ffn_kernel(h_ref, wg_ref, wu_ref, wd_ref, x_ref, mod_ref, o_ref, acc_ref):
    f = pl.program_id(1)
    h = h_ref[...]
    g = jnp.dot(h, wg_ref[...], preferred_element_type=F32)
    u = jnp.dot(h, wu_ref[...], preferred_element_type=F32)
    a = (jax.nn.silu(g) * u).astype(BF16)
    last = pl.num_programs(1) - 1

    @pl.when(f == 0)
    def _():
        acc_ref[...] = jnp.dot(a, wd_ref[...], preferred_element_type=F32)

    @pl.when((f > 0) & (f < last))
    def _():
        acc_ref[...] += jnp.dot(a, wd_ref[...], preferred_element_type=F32)

    @pl.when(f == last)
    def _():
        acc = acc_ref[...] + jnp.dot(a, wd_ref[...], preferred_element_type=F32)
        o_ref[...] = x_ref[...] + mod_ref[0, 5:6, :] * acc


def _ffn(h2, w_gate, w_up, w_down, x, mod, seq_len, tm, tf):
    n, d = x.shape
    d_ff = w_gate.shape[1]
    assert d_ff // tf >= 2
    tm, tiles_per_seq = _token_tiling(n, seq_len, mod.shape[0], tm)
    mod_map = lambda i, f: (i // tiles_per_seq, 0, 0)
    return pl.pallas_call(
        _ffn_kernel,
        grid=(n // tm, d_ff // tf),
        in_specs=[pl.BlockSpec((tm, d), lambda i, f: (i, 0)),
                  pl.BlockSpec((d, tf), lambda i, f: (0, f)),
                  pl.BlockSpec((d, tf), lambda i, f: (0, f)),
                  pl.BlockSpec((tf, d), lambda i, f: (f, 0)),
                  pl.BlockSpec((tm, d), lambda i, f: (i, 0)),
                  pl.BlockSpec((1, N_MOD, d), mod_map)],
        out_specs=pl.BlockSpec((tm, d), lambda i, f: (i, 0)),
        out_shape=jax.ShapeDtypeStruct((n, d), F32),
        scratch_shapes=[pltpu.VMEM((tm, d), F32)],
        compiler_params=_cparams(("arbitrary", "arbitrary"), 56),
        name="swiglu_ffn",
    )(h2, w_gate, w_up, w_down, x, mod)


def _rope_tables(n_tokens):
    pairs = HEAD_DIM // 4
    t = jnp.arange(n_tokens)
    row = (t // GRID_W).astype(F32)
    col = (t % GRID_W).astype(F32)
    inv = jnp.power(ROPE_BASE, -jnp.arange(pairs, dtype=F32) / pairs)
    ang = jnp.concatenate([row[:, None] * inv, col[:, None] * inv], axis=-1)
    cos = jnp.cos(ang)
    sin = jnp.sin(ang)
    return jnp.concatenate([cos, cos], axis=-1), jnp.concatenate([-sin, sin], axis=-1)


def _values_t_with_ones(cache_v):
    b, depth, past, n_kv, d = cache_v.shape
    vt = jnp.transpose(cache_v.astype(BF16), (0, 1, 3, 4, 2))
    ones = jnp.ones((b, depth, n_kv, VT_ROWS - d, past), BF16)
    return jnp.concatenate([vt, ones], axis=3).reshape(b, depth, n_kv * VT_ROWS, past)


def _trunk_layer(x, mod, lw, seq_len, rope_tabs=None, ctx_kv=None):
    (w_in, w_out, w_gate, w_up, w_down, g1, g2, gains, sink, conv_w, conv_b, cln_g, cln_b) = lw
    latent = ctx_kv is not None
    proj = _in_projection(x, mod, g1, w_in, gains, rope_tabs, seq_len, tm=512)
    z, cz = proj[0], proj[1]
    if latent:
        ka_c, va_c, kb_c, vb_c = ctx_kv
        oa = _win_attention(z, ka_c, va_c, sink, seq_len, tq=512)
        ob = _glob_attention(z, proj[2], kb_c, vb_c, seq_len, tq=256, tk=256)
    else:
        oa, ob = _ctx_attention(z, sink, seq_len)
    oc = _conformer_conv(cz, conv_w, conv_b, cln_g, cln_b, seq_len, tc=512)
    x, h2 = _out_projection(oa, ob, oc, x, mod, g2, w_out, seq_len, tm=512)
    x = _ffn(h2, w_gate, w_up, w_down, x, mod, seq_len, tm=512, tf=512)
    return x, proj[2:]


def kernel(x_prompt, x_sample, cache_a_k, cache_a_v, cache_b_k, cache_b_v, c, c_ctx, w_ada, b_ada, w_in, w_out, w_gate, w_up, w_down, norm1_g, norm2_g, qnorm_a_g, knorm_a_g, qnorm_b_g, knorm_b_g, sink_a, conv_w, conv_b, conv_ln_g, conv_ln_b):
    batch, seq, d = x_prompt.shape
    dec_batch, dec_seq, _ = x_sample.shape
    depth = w_in.shape[0]
    past = cache_a_k.shape[2]

    n_rows = 1 + dec_batch
    pad_rows = -n_rows % 8
    cvecs = jnp.concatenate([c_ctx[None, :], c, jnp.zeros((pad_rows, d), F32)], axis=0)
    mods = _ada_mods(cvecs, w_ada, b_ada).reshape(depth, n_rows + pad_rows, N_MOD, d)

    rope_tabs = _rope_tables(dec_seq)
    w_in_b, w_out_b = w_in.astype(BF16), w_out.astype(BF16)
    w_gate_b, w_up_b, w_down_b = w_gate.astype(BF16), w_up.astype(BF16), w_down.astype(BF16)
    ctx_kv = [t.astype(BF16).reshape(dec_batch, depth, past, -1)
              for t in (cache_a_k, cache_a_v, cache_b_k, cache_b_v)]
    ctx_kv[3] = _values_t_with_ones(cache_b_v)

    y_prompt = x_prompt.reshape(batch * seq, d)
    y_sample = x_sample.reshape(dec_batch * dec_seq, d)
    new_kv = []
    for l in range(depth):
        gains = jnp.stack([qnorm_a_g[l], knorm_a_g[l], qnorm_b_g[l], knorm_b_g[l]])
        lw = (w_in_b[l], w_out_b[l], w_gate_b[l], w_up_b[l], w_down_b[l],
              norm1_g[l][None, :], norm2_g[l][None, :], gains, sink_a[l],
              conv_w[l], conv_b[l][None, :], conv_ln_g[l][None, :], conv_ln_b[l][None, :])
        y_prompt, kv = _trunk_layer(y_prompt, mods[l, 0:1], lw, seq)
        new_kv.append(kv)
        y_sample, _ = _trunk_layer(y_sample, mods[l, 1:n_rows], lw, dec_seq, rope_tabs,
                                   [t[:, l] for t in ctx_kv])

    def stacked(i, n_kv):
        return jnp.stack([kv[i].reshape(batch, seq, n_kv, HEAD_DIM) for kv in new_kv], axis=1)

    return (y_prompt.reshape(batch, seq, d), y_sample.reshape(dec_batch, dec_seq, d),
            stacked(0, A_KV), stacked(1, A_KV), stacked(2, B_KV), stacked(3, B_KV))
```

```python
import functools

import jax
import jax.numpy as jnp
from jax import lax
from jax.experimental import pallas as pl
from jax.experimental.pallas import tpu as pltpu

F32 = jnp.float32
BF16 = jnp.bfloat16

HEAD_DIM = 128
GRID_W = 64
A_HEADS = 4
A_KV = 2
B_HEADS = 8
B_KV = 2
C_CH = 512
CONV_W = 31
WINDOW = 128
ROPE_BASE = 10000.0
N_MOD = 6
EPS = 1e-6

A_Q_W = A_HEADS * HEAD_DIM
A_KV_W = A_KV * HEAD_DIM
B_Q_W = B_HEADS * HEAD_DIM
B_KV_W = B_KV * HEAD_DIM
QKV_W = A_Q_W + 2 * A_KV_W + B_Q_W + 2 * B_KV_W
CZ_W = 2 * C_CH
COL_KA = A_Q_W
COL_VA = COL_KA + A_KV_W
COL_QB = COL_VA + A_KV_W
COL_KB = COL_QB + B_Q_W
COL_VB = COL_KB + B_KV_W

V7X_VMEM_BYTES = 64 * 1024 * 1024
MASK_VALUE = -1e30
LOG2_E = 1.4426950408889634
CONV_HALO = 16
CONV_ROWS = 32
CONV_UNROLL = 4
ATTN_LOOKAHEAD = 14
SUBLANES = 8
BF16_ROWS = 16
VT_ROWS = HEAD_DIM + BF16_ROWS
FFN_TILE = 1408


def _cparams(semantics, vmem_mb):
    assert vmem_mb * 1024 * 1024 < V7X_VMEM_BYTES
    return pltpu.CompilerParams(dimension_semantics=semantics,
                                vmem_limit_bytes=vmem_mb * 1024 * 1024)


def _rms(x, g):
    return x * lax.rsqrt(jnp.mean(x * x, axis=-1, keepdims=True) + EPS) * g


def _qk_t(q, k):
    return lax.dot_general(q, k, (((1,), (1,)), ((), ())), preferred_element_type=F32)


def _token_tiling(n_tokens, seq_len, n_mod_rows, tm):
    span = n_tokens if n_mod_rows == 1 else seq_len
    tm = min(tm, span)
    assert span % tm == 0
    return tm, span // tm


def _mods_kernel(c_ref, w_ref, b_ref, o_ref):
    a = jax.nn.silu(c_ref[...]).astype(BF16)
    o_ref[0] = jnp.dot(a, w_ref[0].astype(BF16), preferred_element_type=F32) + b_ref[0]


def _ada_mods(cvecs, w_ada, b_ada):
    depth, d, n = w_ada.shape
    r = cvecs.shape[0]
    tn = 1024
    return pl.pallas_call(
        _mods_kernel,
        grid=(depth, n // tn),
        in_specs=[pl.BlockSpec((r, d), lambda l, j: (0, 0)),
                  pl.BlockSpec((1, d, tn), lambda l, j: (l, 0, j)),
                  pl.BlockSpec((1, 1, tn), lambda l, j: (l, 0, j))],
        out_specs=pl.BlockSpec((1, r, tn), lambda l, j: (l, 0, j)),
        out_shape=jax.ShapeDtypeStruct((depth, r, n), F32),
        compiler_params=_cparams(("arbitrary", "arbitrary"), 40),
        name="ada_mods",
    )(cvecs, w_ada, b_ada.reshape(depth, 1, n))


def _inproj_kernel(*refs, rope, emit_kv, emit_vt):
    x_ref, mod_ref, g1_ref, w_ref, gains_ref = refs[:5]
    refs = refs[5:]
    if rope:
        cos_ref, sin_ref = refs[:2]
        refs = refs[2:]
    z_ref, cz_ref = refs[:2]
    refs = refs[2:]
    if emit_kv:
        ka_ref, va_ref, kb_ref, vb_ref = refs[:4]
        refs = refs[4:]
    if emit_vt:
        vt_ref, = refs

    h = _rms(x_ref[...], g1_ref[...])
    h = h * (1.0 + mod_ref[0, 1:2, :]) + mod_ref[0, 0:1, :]
    hb = h.astype(BF16)
    if rope:
        cos = cos_ref[...]
        sin = sin_ref[...]
    q_scale = HEAD_DIM ** -0.5
    q_scale_b = q_scale * LOG2_E

    def normed(zh, gain_idx):
        return _rms(zh, gains_ref[gain_idx:gain_idx + 1, :])

    def rotated(zh):
        if not rope:
            return zh
        return zh * cos + pltpu.roll(zh, HEAD_DIM // 2, axis=1) * sin

    chunk = 512
    for c0 in range(0, QKV_W, chunk):
        zc = jnp.dot(hb, w_ref[:, c0:c0 + chunk], preferred_element_type=F32)
        for j in range(chunk // HEAD_DIM):
            col = c0 + j * HEAD_DIM
            zh = zc[:, j * HEAD_DIM:(j + 1) * HEAD_DIM]
            if col < COL_KA:
                out = rotated(normed(zh, 0)) * q_scale
            elif col < COL_VA:
                kn = normed(zh, 1)
                if emit_kv:
                    ka_ref[:, col - COL_KA:col - COL_KA + HEAD_DIM] = kn
                out = rotated(kn)
            elif col < COL_QB:
                if emit_kv:
                    va_ref[:, col - COL_VA:col - COL_VA + HEAD_DIM] = zh
                out = zh
            elif col < COL_KB:
                out = rotated(normed(zh, 2)) * q_scale_b
            elif col < COL_VB:
                kn = normed(zh, 3)
                if emit_kv:
                    kb_ref[:, col - COL_KB:col - COL_KB + HEAD_DIM] = kn
                out = rotated(kn)
            else:
                if emit_kv:
                    vb_ref[:, col - COL_VB:col - COL_VB + HEAD_DIM] = zh
                if emit_vt:
                    r0 = (col - COL_VB) // HEAD_DIM * VT_ROWS
                    vt_ref[r0:r0 + HEAD_DIM, :] = zh.T.astype(BF16)
                    vt_ref[r0 + HEAD_DIM:r0 + VT_ROWS, :] = jnp.ones((BF16_ROWS, zh.shape[0]), BF16)
                out = zh
            z_ref[:, col:col + HEAD_DIM] = out.astype(BF16)
    for c0 in range(0, CZ_W, chunk):
        cz_ref[:, c0:c0 + chunk] = jnp.dot(hb, w_ref[:, QKV_W + c0:QKV_W + c0 + chunk],
                                           preferred_element_type=F32)


def _in_projection(x, mod, g1, w_in, gains, rope_tabs, seq_len, tm):
    n, d = x.shape
    tm, tiles_per_seq = _token_tiling(n, seq_len, mod.shape[0], tm)
    mod_map = lambda i: (i // tiles_per_seq, 0, 0)
    const = lambda i: (0, 0)
    in_specs = [pl.BlockSpec((tm, d), lambda i: (i, 0)),
                pl.BlockSpec((1, N_MOD, d), mod_map),
                pl.BlockSpec((1, d), const),
                pl.BlockSpec(w_in.shape, const, pipeline_mode=pl.Buffered(1)),
                pl.BlockSpec(gains.shape, const)]
    args = [x, mod, g1, w_in, gains]
    rope = rope_tabs is not None
    emit_kv, emit_vt = not rope, rope
    if rope:
        in_specs += [pl.BlockSpec((tm, HEAD_DIM), lambda i: (i % tiles_per_seq, 0))] * 2
        args += list(rope_tabs)
    out_specs = [pl.BlockSpec((tm, QKV_W), lambda i: (i, 0)),
                 pl.BlockSpec((tm, CZ_W), lambda i: (i, 0))]
    out_shape = [jax.ShapeDtypeStruct((n, QKV_W), BF16), jax.ShapeDtypeStruct((n, CZ_W), F32)]
    if emit_kv:
        out_specs += [pl.BlockSpec((tm, A_KV_W), lambda i: (i, 0))] * 4
        out_shape += [jax.ShapeDtypeStruct((n, A_KV_W), F32)] * 4
    if emit_vt:
        out_specs += [pl.BlockSpec((B_KV * VT_ROWS, tm), lambda i: (0, i))]
        out_shape += [jax.ShapeDtypeStruct((B_KV * VT_ROWS, n), BF16)]
    return pl.pallas_call(
        functools.partial(_inproj_kernel, rope=rope, emit_kv=emit_kv, emit_vt=emit_vt),
        grid=(n // tm,),
        in_specs=in_specs,
        out_specs=out_specs,
        out_shape=out_shape,
        compiler_params=_cparams(("arbitrary",), 56),
        name="in_projection_lat" if rope else "in_projection_ctx",
    )(*args)


def _softmax_pv(s_list, v_list, sink_col, exp=jnp.exp):
    m = s_list[0].max(axis=-1, keepdims=True)
    for s in s_list[1:]:
        m = jnp.maximum(m, s.max(axis=-1, keepdims=True))
    if sink_col is not None:
        m = jnp.maximum(m, sink_col)
    denom = None
    acc = None
    for s, v in zip(s_list, v_list):
        p = exp(s - m)
        ps = p.sum(axis=-1, keepdims=True)
        pv = jnp.dot(p.astype(BF16), v, preferred_element_type=F32)
        denom = ps if denom is None else denom + ps
        acc = pv if acc is None else acc + pv
    if sink_col is not None:
        denom = denom + exp(sink_col - m)
    return acc * (1.0 / denom)


def _stack_heads(ref, col0, n_heads):
    return jnp.concatenate([ref[:, col0 + g * HEAD_DIM:col0 + (g + 1) * HEAD_DIM]
                            for g in range(n_heads)], axis=0)


def _sink_column(sink_ref, head0, n_heads, rows_per_head):
    row = lax.broadcasted_iota(jnp.int32, (n_heads * rows_per_head, 1), 0)
    col = jnp.full((n_heads * rows_per_head, 1), sink_ref[head0], F32)
    for g in range(1, n_heads):
        col = jnp.where(row >= g * rows_per_head, sink_ref[head0 + g], col)
    return col


def _ctx_attn_kernel(sink_ref, z_ref, oa_ref, ob_ref):
    t = z_ref.shape[0]
    ga = A_HEADS // A_KV
    for kh in range(A_KV):
        q = _stack_heads(z_ref, kh * ga * HEAD_DIM, ga)
        k = z_ref[:, COL_KA + kh * HEAD_DIM:COL_KA + (kh + 1) * HEAD_DIM]
        v = z_ref[:, COL_VA + kh * HEAD_DIM:COL_VA + (kh + 1) * HEAD_DIM]
        o = _softmax_pv([_qk_t(q, k)], [v], _sink_column(sink_ref, kh * ga, ga, t))
        for g in range(ga):
            c = (kh * ga + g) * HEAD_DIM
            oa_ref[:, c:c + HEAD_DIM] = o[g * t:(g + 1) * t].astype(BF16)
    gb = B_HEADS // B_KV
    for kh in range(B_KV):
        q = _stack_heads(z_ref, COL_QB + kh * gb * HEAD_DIM, gb)
        k = z_ref[:, COL_KB + kh * HEAD_DIM:COL_KB + (kh + 1) * HEAD_DIM]
        v = z_ref[:, COL_VB + kh * HEAD_DIM:COL_VB + (kh + 1) * HEAD_DIM]
        o = _softmax_pv([_qk_t(q, k)], [v], None, exp=jnp.exp2)
        for g in range(gb):
            c = (kh * gb + g) * HEAD_DIM
            ob_ref[:, c:c + HEAD_DIM] = o[g * t:(g + 1) * t].astype(BF16)


def _ctx_attention(z, sink, seq_len):
    n = z.shape[0]
    return pl.pallas_call(
        _ctx_attn_kernel,
        grid=(n // seq_len,),
        in_specs=[pl.BlockSpec(memory_space=pltpu.SMEM),
                  pl.BlockSpec((seq_len, QKV_W), lambda b: (b, 0))],
        out_specs=[pl.BlockSpec((seq_len, A_Q_W), lambda b: (b, 0)),
                   pl.BlockSpec((seq_len, B_Q_W), lambda b: (b, 0))],
        out_shape=[jax.ShapeDtypeStruct((n, A_Q_W), BF16), jax.ShapeDtypeStruct((n, B_Q_W), BF16)],
        compiler_params=_cparams(("arbitrary",), 32),
        name="ctx_attention",
    )(sink, z)


def _win_attn_kernel(sink_ref, q_ref, kp_ref, kc_ref, kn_ref, vp_ref, vc_ref, vn_ref,
                     kx_ref, vx_ref, oa_ref, *, seq_len):
    tq = q_ref.shape[0]
    t0 = pl.program_id(1) * tq
    ga = A_HEADS // A_KV
    nk = tq + 2 * WINDOW
    row = lax.broadcasted_iota(jnp.int32, (ga * tq, nk), 0) & (tq - 1)
    col = lax.broadcasted_iota(jnp.int32, (ga * tq, nk), 1) - WINDOW
    key_pos = col + t0
    valid = (jnp.abs(col - row) <= WINDOW) & (key_pos >= 0) & (key_pos < seq_len)
    for kh in range(A_KV):
        hs = slice(kh * HEAD_DIM, (kh + 1) * HEAD_DIM)
        q = _stack_heads(q_ref, kh * ga * HEAD_DIM, ga)
        kw = jnp.concatenate([kp_ref[:, hs], kc_ref[:, hs], kn_ref[:, hs]], axis=0)
        vw = jnp.concatenate([vp_ref[:, hs], vc_ref[:, hs], vn_ref[:, hs]], axis=0)
        s_win = jnp.where(valid, _qk_t(q, kw), MASK_VALUE)
        s_ctx = _qk_t(q, kx_ref[0, :, hs])
        o = _softmax_pv([s_win, s_ctx], [vw, vx_ref[0, :, hs]],
                        _sink_column(sink_ref, kh * ga, ga, tq))
        for g in range(ga):
            c = (kh * ga + g) * HEAD_DIM
            oa_ref[:, c:c + HEAD_DIM] = o[g * tq:(g + 1) * tq].astype(BF16)


def _win_attention(z, kx, vx, sink, seq_len, tq):
    n = z.shape[0]
    nb = n // seq_len
    tq = min(tq, seq_len)
    assert tq & (tq - 1) == 0 and tq % WINDOW == 0
    nt = seq_len // tq
    halos_per_tile = tq // WINDOW
    n_halo = n // WINDOW
    past = kx.shape[1]
    cur = lambda cb: pl.BlockSpec((tq, A_KV_W), lambda b, t: (b * nt + t, cb))
    prev = lambda cb: pl.BlockSpec(
        (WINDOW, A_KV_W), lambda b, t: (jnp.maximum((b * nt + t) * halos_per_tile - 1, 0), cb))
    nxt = lambda cb: pl.BlockSpec(
        (WINDOW, A_KV_W),
        lambda b, t: (jnp.minimum((b * nt + t + 1) * halos_per_tile, n_halo - 1), cb))
    kcb = COL_KA // A_KV_W
    vcb = COL_VA // A_KV_W
    ctx = pl.BlockSpec((1, past, A_KV_W), lambda b, t: (b, 0, 0))
    return pl.pallas_call(
        functools.partial(_win_attn_kernel, seq_len=seq_len),
        grid=(nb, nt),
        in_specs=[pl.BlockSpec(memory_space=pltpu.SMEM),
                  pl.BlockSpec((tq, A_Q_W), lambda b, t: (b * nt + t, 0)),
                  prev(kcb), cur(kcb), nxt(kcb), prev(vcb), cur(vcb), nxt(vcb), ctx, ctx],
        out_specs=pl.BlockSpec((tq, A_Q_W), lambda b, t: (b * nt + t, 0)),
        out_shape=jax.ShapeDtypeStruct((n, A_Q_W), BF16),
        compiler_params=_cparams(("arbitrary", "arbitrary"), 40),
        name="win_attention",
    )(sink, z, z, z, z, z, z, z, kx, vx)


def _glob_attn_kernel(q_ref, kx_ref, vxt_ref, kl_ref, vlt_ref, ob_ref, *, tk):
    gb = B_HEADS // B_KV
    past = kx_ref.shape[1]
    assert past % tk == 0 and kl_ref.shape[0] % tk == 0
    heads = [q_ref[:, g * HEAD_DIM:(g + 1) * HEAD_DIM] for g in range(gb)]

    def update(st, vt, carry):
        c_max = st.max(axis=0, keepdims=True)
        if carry is None:
            m = c_max
            pt = jnp.exp2(st - m)
            return m, jnp.dot(vt, pt.astype(BF16), preferred_element_type=F32)
        m_old, acc = carry
        m = jnp.maximum(m_old, c_max)
        alpha = jnp.exp2(m_old - m)
        pt = jnp.exp2(st - m)
        acc = alpha * acc + jnp.dot(vt, pt.astype(BF16), preferred_element_type=F32)
        return m, acc

    def keys(i):
        c0 = i * tk
        if c0 < past:
            return kx_ref[0, c0:c0 + tk, :], vxt_ref[0, :, c0:c0 + tk]
        c0 -= past
        return kl_ref[c0:c0 + tk, :], vlt_ref[:, c0:c0 + tk]

    n_tiles = (past + kl_ref.shape[0]) // tk * gb
    score = lambda t: _qk_t(keys(t // gb)[0], heads[t % gb])
    carries = [None] * gb
    pending = [score(t) for t in range(min(ATTN_LOOKAHEAD, n_tiles))]
    for t in range(n_tiles):
        if t + ATTN_LOOKAHEAD < n_tiles:
            pending.append(score(t + ATTN_LOOKAHEAD))
        carries[t % gb] = update(pending.pop(0), keys(t // gb)[1], carries[t % gb])
    for g, (_, acc) in enumerate(carries):
        o = acc[:HEAD_DIM] * (1.0 / acc[HEAD_DIM:HEAD_DIM + 1])
        ob_ref[:, g * HEAD_DIM:(g + 1) * HEAD_DIM] = o.T.astype(BF16)


def _glob_attention(z, vt, kx, vxt, seq_len, tq, tk):
    n = z.shape[0]
    nb = n // seq_len
    tq = min(tq, seq_len)
    tk = min(tk, seq_len)
    nt = seq_len // tq
    past = kx.shape[1]
    q_w = B_Q_W // B_KV
    return pl.pallas_call(
        functools.partial(_glob_attn_kernel, tk=tk),
        grid=(nb, B_KV, nt),
        in_specs=[pl.BlockSpec((tq, q_w), lambda b, kh, t: (b * nt + t, COL_QB // q_w + kh)),
                  pl.BlockSpec((1, past, HEAD_DIM), lambda b, kh, t: (b, 0, kh)),
                  pl.BlockSpec((1, VT_ROWS, past), lambda b, kh, t: (b, kh, 0)),
                  pl.BlockSpec((seq_len, HEAD_DIM), lambda b, kh, t: (b, COL_KB // HEAD_DIM + kh)),
                  pl.BlockSpec((VT_ROWS, seq_len), lambda b, kh, t: (kh, b))],
        out_specs=pl.BlockSpec((tq, q_w), lambda b, kh, t: (b * nt + t, kh)),
        out_shape=jax.ShapeDtypeStruct((n, B_Q_W), BF16),
        compiler_params=_cparams(("arbitrary", "arbitrary", "arbitrary"), 48),
        name="glob_attention",
    )(z, kx, vxt, z, vt)


def _conv_kernel(cp_ref, cc_ref, cn_ref, w_ref, b_ref, g_ref, beta_ref, o_ref, u_ref, *,
                 tiles_per_seq):
    tc = cc_ref.shape[0]
    ti = pl.program_id(0) % tiles_per_seq

    def glu(ref):
        return ref[:, :C_CH] * jax.nn.sigmoid(ref[:, C_CH:])

    u_ref[0:CONV_HALO, :] = jnp.where(ti > 0, glu(cp_ref), 0.0)
    u_ref[CONV_HALO:CONV_HALO + tc, :] = glu(cc_ref)
    u_ref[CONV_HALO + tc:, :] = jnp.where(ti < tiles_per_seq - 1, glu(cn_ref), 0.0)

    bias = b_ref[...]
    gamma = g_ref[...]
    beta = beta_ref[...]
    base = CONV_HALO - CONV_W // 2

    def body(r, carry):
        r0 = pl.multiple_of(r * CONV_ROWS, CONV_ROWS)
        y = None
        for b in range(SUBLANES):
            v = None
            for a in range((base + CONV_W - 1 - b) // SUBLANES + 1):
                k = SUBLANES * a + b - base
                if k < 0:
                    continue
                term = w_ref[k:k + 1, :] * u_ref[pl.ds(r0 + SUBLANES * a, CONV_ROWS + SUBLANES), :]
                v = term if v is None else v + term
            v = v[b:b + CONV_ROWS, :]
            y = v if y is None else y + v
        y = y + bias
        mu = jnp.mean(y, axis=-1, keepdims=True)
        yc = y - mu
        yn = yc * lax.rsqrt(jnp.mean(yc * yc, axis=-1, keepdims=True) + EPS) * gamma + beta
        o_ref[pl.ds(r0, CONV_ROWS), :] = jax.nn.silu(yn).astype(BF16)
        return carry

    lax.fori_loop(0, tc // CONV_ROWS, body, 0, unroll=CONV_UNROLL)


def _conformer_conv(cz, conv_w, conv_b, ln_g, ln_b, seq_len, tc):
    n = cz.shape[0]
    tc = min(tc, seq_len)
    tiles_per_seq = seq_len // tc
    halos_per_tile = tc // CONV_HALO
    n_halo = n // CONV_HALO
    const = lambda i: (0, 0)
    return pl.pallas_call(
        functools.partial(_conv_kernel, tiles_per_seq=tiles_per_seq),
        grid=(n // tc,),
        in_specs=[pl.BlockSpec((CONV_HALO, CZ_W),
                               lambda i: (jnp.maximum(i * halos_per_tile - 1, 0), 0)),
                  pl.BlockSpec((tc, CZ_W), lambda i: (i, 0)),
                  pl.BlockSpec((CONV_HALO, CZ_W),
                               lambda i: (jnp.minimum((i + 1) * halos_per_tile, n_halo - 1), 0)),
                  pl.BlockSpec((CONV_W, C_CH), const),
                  pl.BlockSpec((1, C_CH), const),
                  pl.BlockSpec((1, C_CH), const),
                  pl.BlockSpec((1, C_CH), const)],
        out_specs=pl.BlockSpec((tc, C_CH), lambda i: (i, 0)),
        out_shape=jax.ShapeDtypeStruct((n, C_CH), BF16),
        scratch_shapes=[pltpu.VMEM((tc + 2 * CONV_HALO, C_CH), F32)],
        compiler_params=_cparams(("arbitrary",), 32),
        name="conformer_conv",
    )(cz, cz, cz, conv_w, conv_b, ln_g, ln_b)


def _outproj_kernel(oa_ref, ob_ref, oc_ref, x_ref, mod_ref, g2_ref, w_ref, xo_ref, h2_ref):
    mix = jnp.concatenate([oa_ref[...], ob_ref[...], oc_ref[...]], axis=1)
    y = jnp.dot(mix, w_ref[...], preferred_element_type=F32)
    x = x_ref[...] + mod_ref[0, 2:3, :] * y
    xo_ref[...] = x
    h2 = _rms(x, g2_ref[...])
    h2_ref[...] = (h2 * (1.0 + mod_ref[0, 4:5, :]) + mod_ref[0, 3:4, :]).astype(BF16)


def _out_projection(oa, ob, oc, x, mod, g2, w_out, seq_len, tm):
    n, d = x.shape
    tm, tiles_per_seq = _token_tiling(n, seq_len, mod.shape[0], tm)
    mod_map = lambda i: (i // tiles_per_seq, 0, 0)
    const = lambda i: (0, 0)
    row = lambda w: pl.BlockSpec((tm, w), lambda i: (i, 0))
    return pl.pallas_call(
        _outproj_kernel,
        grid=(n // tm,),
        in_specs=[row(A_Q_W), row(B_Q_W), row(C_CH), row(d),
                  pl.BlockSpec((1, N_MOD, d), mod_map),
                  pl.BlockSpec((1, d), const),
                  pl.BlockSpec(w_out.shape, const, pipeline_mode=pl.Buffered(1))],
        out_specs=[row(d), row(d)],
        out_shape=[jax.ShapeDtypeStruct((n, d), F32), jax.ShapeDtypeStruct((n, d), BF16)],
        compiler_params=_cparams(("arbitrary",), 56),
        name="out_projection",
    )(oa, ob, oc, x, mod, g2, w_out)


def _ffn_gate_kernel(h_ref, wgu_ref, a_ref):
    tf = a_ref.shape[1]
    gu = jnp.dot(h_ref[...], wgu_ref[...], preferred_element_type=F32)
    a_ref[...] = (jax.nn.silu(gu[:, :tf]) * gu[:, tf:]).astype(BF16)


def _ffn_down_kernel(a_ref, wd_ref, x_ref, mod_ref, o_ref):
    y = jnp.dot(a_ref[...], wd_ref[...], preferred_element_type=F32)
    o_ref[...] = x_ref[...] + mod_ref[0, 5:6, :] * y


def _ffn(h2, w_gate_up, w_down, x, mod, seq_len, tm_gate, tm_down):
    n, d = x.shape
    d_ff = w_down.shape[0]
    tf = FFN_TILE
    tm_gate = min(tm_gate, n)
    a = pl.pallas_call(
        _ffn_gate_kernel,
        grid=(n // tm_gate, d_ff // tf),
        in_specs=[pl.BlockSpec((tm_gate, d), lambda i, f: (i, 0)),
                  pl.BlockSpec((d, 2 * tf), lambda i, f: (0, f))],
        out_specs=pl.BlockSpec((tm_gate, tf), lambda i, f: (i, f)),
        out_shape=jax.ShapeDtypeStruct((n, d_ff), BF16),
        compiler_params=_cparams(("arbitrary", "arbitrary"), 56),
        name="ffn_gate_up",
    )(h2, w_gate_up)
    tm, tiles_per_seq = _token_tiling(n, seq_len, mod.shape[0], tm_down)
    return pl.pallas_call(
        _ffn_down_kernel,
        grid=(n // tm,),
        in_specs=[pl.BlockSpec((tm, d_ff), lambda i: (i, 0)),
                  pl.BlockSpec(w_down.shape, lambda i: (0, 0), pipeline_mode=pl.Buffered(1)),
                  pl.BlockSpec((tm, d), lambda i: (i, 0)),
                  pl.BlockSpec((1, N_MOD, d), lambda i: (i // tiles_per_seq, 0, 0))],
        out_specs=pl.BlockSpec((tm, d), lambda i: (i, 0)),
        out_shape=jax.ShapeDtypeStruct((n, d), F32),
        compiler_params=_cparams(("arbitrary",), 56),
        name="ffn_down",
    )(a, w_down, x, mod)


def _rope_tables(n_tokens):
    pairs = HEAD_DIM // 4
    t = jnp.arange(n_tokens)
    row = (t // GRID_W).astype(F32)
    col = (t % GRID_W).astype(F32)
    inv = jnp.power(ROPE_BASE, -jnp.arange(pairs, dtype=F32) / pairs)
    ang = jnp.concatenate([row[:, None] * inv, col[:, None] * inv], axis=-1)
    cos = jnp.cos(ang)
    sin = jnp.sin(ang)
    return jnp.concatenate([cos, cos], axis=-1), jnp.concatenate([-sin, sin], axis=-1)


def _interleave_gate_up(w_gate, w_up):
    depth, d, d_ff = w_gate.shape
    assert d_ff % FFN_TILE == 0
    tiled = lambda w: w.astype(BF16).reshape(depth, d, d_ff // FFN_TILE, 1, FFN_TILE)
    return jnp.concatenate([tiled(w_gate), tiled(w_up)], axis=3).reshape(depth, d, 2 * d_ff)


def _values_t_with_ones(cache_v):
    b, depth, past, n_kv, d = cache_v.shape
    vt = jnp.transpose(cache_v.astype(BF16), (0, 1, 3, 4, 2))
    ones = jnp.ones((b, depth, n_kv, VT_ROWS - d, past), BF16)
    return jnp.concatenate([vt, ones], axis=3).reshape(b, depth, n_kv * VT_ROWS, past)


def _trunk_layer(x, mod, lw, seq_len, rope_tabs=None, ctx_kv=None):
    (w_in, w_out, w_gate_up, w_down, g1, g2, gains, sink, conv_w, conv_b, cln_g, cln_b) = lw
    latent = ctx_kv is not None
    proj = _in_projection(x, mod, g1, w_in, gains, rope_tabs, seq_len, tm=512)
    z, cz = proj[0], proj[1]
    if latent:
        ka_c, va_c, kb_c, vb_c = ctx_kv
        oa = _win_attention(z, ka_c, va_c, sink, seq_len, tq=512)
        ob = _glob_attention(z, proj[2], kb_c, vb_c, seq_len, tq=256, tk=256)
    else:
        oa, ob = _ctx_attention(z, sink, seq_len)
    oc = _conformer_conv(cz, conv_w, conv_b, cln_g, cln_b, seq_len, tc=512)
    x, h2 = _out_projection(oa, ob, oc, x, mod, g2, w_out, seq_len, tm=512)
    x = _ffn(h2, w_gate_up, w_down, x, mod, seq_len, tm_gate=1024, tm_down=512)
    return x, proj[2:]


def kernel(x_prompt, x_sample, cache_a_k, cache_a_v, cache_b_k, cache_b_v, c, c_ctx, w_ada, b_ada, w_in, w_out, w_gate, w_up, w_down, norm1_g, norm2_g, qnorm_a_g, knorm_a_g, qnorm_b_g, knorm_b_g, sink_a, conv_w, conv_b, conv_ln_g, conv_ln_b):
    batch, seq, d = x_prompt.shape
    dec_batch, dec_seq, _ = x_sample.shape
    depth = w_in.shape[0]
    past = cache_a_k.shape[2]

    n_rows = 1 + dec_batch
    pad_rows = -n_rows % 8
    cvecs = jnp.concatenate([c_ctx[None, :], c, jnp.zeros((pad_rows, d), F32)], axis=0)
    mods = _ada_mods(cvecs, w_ada, b_ada).reshape(depth, n_rows + pad_rows, N_MOD, d)

    rope_tabs = _rope_tables(dec_seq)
    w_in_b, w_out_b = w_in.astype(BF16), w_out.astype(BF16)
    w_gate_up_b, w_down_b = _interleave_gate_up(w_gate, w_up), w_down.astype(BF16)
    ctx_kv = [t.astype(BF16).reshape(dec_batch, depth, past, -1)
              for t in (cache_a_k, cache_a_v, cache_b_k, cache_b_v)]
    ctx_kv[3] = _values_t_with_ones(cache_b_v)

    y_prompt = x_prompt.reshape(batch * seq, d)
    y_sample = x_sample.reshape(dec_batch * dec_seq, d)
    new_kv = []
    for l in range(depth):
        gains = jnp.stack([qnorm_a_g[l], knorm_a_g[l], qnorm_b_g[l], knorm_b_g[l]])
        lw = (w_in_b[l], w_out_b[l], w_gate_up_b[l], w_down_b[l],
              norm1_g[l][None, :], norm2_g[l][None, :], gains, sink_a[l],
              conv_w[l], conv_b[l][None, :], conv_ln_g[l][None, :], conv_ln_b[l][None, :])
        y_prompt, kv = _trunk_layer(y_prompt, mods[l, 0:1], lw, seq)
        new_kv.append(kv)
        y_sample, _ = _trunk_layer(y_sample, mods[l, 1:n_rows], lw, dec_seq, rope_tabs,
                                   [t[:, l] for t in ctx_kv])

    def stacked(i, n_kv):
        return jnp.stack([kv[i].reshape(batch, seq, n_kv, HEAD_DIM) for kv in new_kv], axis=1)

    return (y_prompt.reshape(batch, seq, d), y_sample.reshape(dec_batch, dec_seq, d),
            stacked(0, A_KV), stacked(1, A_KV), stacked(2, B_KV), stacked(3, B_KV))
```

```python
import functools

import jax
import jax.numpy as jnp
from jax import lax
from jax.experimental import pallas as pl
from jax.experimental.pallas import tpu as pltpu

F32 = jnp.float32
BF16 = jnp.bfloat16

HEAD_DIM = 128
GRID_W = 64
A_HEADS = 4
A_KV = 2
B_HEADS = 8
B_KV = 2
C_CH = 512
CONV_W = 31
WINDOW = 128
ROPE_BASE = 10000.0
N_MOD = 6
EPS = 1e-6

A_Q_W = A_HEADS * HEAD_DIM
A_KV_W = A_KV * HEAD_DIM
B_Q_W = B_HEADS * HEAD_DIM
B_KV_W = B_KV * HEAD_DIM
QKV_W = A_Q_W + 2 * A_KV_W + B_Q_W + 2 * B_KV_W
CZ_W = 2 * C_CH
COL_KA = A_Q_W
COL_VA = COL_KA + A_KV_W
COL_QB = COL_VA + A_KV_W
COL_KB = COL_QB + B_Q_W
COL_VB = COL_KB + B_KV_W

V7X_VMEM_BYTES = 64 * 1024 * 1024
MASK_VALUE = -1e30
LOG2_E = 1.4426950408889634
CONV_HALO = 16
CONV_ROWS = 32
CONV_UNROLL = 4
ATTN_QBLK = 256
ATTN_LOOKAHEAD = 14
SUBLANES = 8
BF16_ROWS = 16
VT_ROWS = HEAD_DIM + BF16_ROWS
FFN_TILE = 1408


def _cparams(semantics, vmem_mb):
    assert vmem_mb * 1024 * 1024 < V7X_VMEM_BYTES
    return pltpu.CompilerParams(dimension_semantics=semantics,
                                vmem_limit_bytes=vmem_mb * 1024 * 1024)


def _rms(x, g):
    return x * lax.rsqrt(jnp.mean(x * x, axis=-1, keepdims=True) + EPS) * g


def _qk_t(q, k):
    return lax.dot_general(q, k, (((1,), (1,)), ((), ())), preferred_element_type=F32)


def _resident_layer_spec(stack, layer):
    _, rows, cols = stack.shape
    return pl.BlockSpec((None, rows, cols), lambda *_: (layer, 0, 0), pipeline_mode=pl.Buffered(1))


def _token_tiling(n_tokens, seq_len, n_mod_rows, tm):
    span = n_tokens if n_mod_rows == 1 else seq_len
    tm = min(tm, span)
    assert span % tm == 0
    return tm, span // tm


def _mods_kernel(c_ref, w_ref, b_ref, o_ref):
    a = jax.nn.silu(c_ref[...]).astype(BF16)
    o_ref[0] = jnp.dot(a, w_ref[0].astype(BF16), preferred_element_type=F32) + b_ref[0]


def _ada_mods(cvecs, w_ada, b_ada):
    depth, d, n = w_ada.shape
    r = cvecs.shape[0]
    tn = 1024
    return pl.pallas_call(
        _mods_kernel,
        grid=(depth, n // tn),
        in_specs=[pl.BlockSpec((r, d), lambda l, j: (0, 0)),
                  pl.BlockSpec((1, d, tn), lambda l, j: (l, 0, j)),
                  pl.BlockSpec((1, 1, tn), lambda l, j: (l, 0, j))],
        out_specs=pl.BlockSpec((1, r, tn), lambda l, j: (l, 0, j)),
        out_shape=jax.ShapeDtypeStruct((depth, r, n), F32),
        compiler_params=_cparams(("arbitrary", "arbitrary"), 40),
        name="ada_mods",
    )(cvecs, w_ada, b_ada.reshape(depth, 1, n))


def _inproj_kernel(*refs, rope, emit_kv, emit_vt):
    x_ref, mod_ref, g1_ref, w_ref, gains_ref = refs[:5]
    refs = refs[5:]
    if rope:
        cos_ref, sin_ref = refs[:2]
        refs = refs[2:]
    z_ref, cz_ref = refs[:2]
    refs = refs[2:]
    if emit_kv:
        ka_ref, va_ref, kb_ref, vb_ref = refs[:4]
        refs = refs[4:]
    if emit_vt:
        vt_ref, = refs

    h = _rms(x_ref[...], g1_ref[...])
    h = h * (1.0 + mod_ref[0, 1:2, :]) + mod_ref[0, 0:1, :]
    hb = h.astype(BF16)
    if rope:
        cos = cos_ref[...]
        sin = sin_ref[...]
    q_scale = HEAD_DIM ** -0.5
    q_scale_b = q_scale * LOG2_E

    def normed(zh, gain_idx):
        return _rms(zh, gains_ref[gain_idx:gain_idx + 1, :])

    def rotated(zh):
        if not rope:
            return zh
        return zh * cos + pltpu.roll(zh, HEAD_DIM // 2, axis=1) * sin

    chunk = 512
    for c0 in range(0, QKV_W, chunk):
        zc = jnp.dot(hb, w_ref[:, c0:c0 + chunk], preferred_element_type=F32)
        for j in range(chunk // HEAD_DIM):
            col = c0 + j * HEAD_DIM
            zh = zc[:, j * HEAD_DIM:(j + 1) * HEAD_DIM]
            if col < COL_KA:
                out = rotated(normed(zh, 0)) * q_scale
            elif col < COL_VA:
                kn = normed(zh, 1)
                if emit_kv:
                    ka_ref[:, col - COL_KA:col - COL_KA + HEAD_DIM] = kn
                out = rotated(kn)
            elif col < COL_QB:
                if emit_kv:
                    va_ref[:, col - COL_VA:col - COL_VA + HEAD_DIM] = zh
                out = zh
            elif col < COL_KB:
                out = rotated(normed(zh, 2)) * q_scale_b
            elif col < COL_VB:
                kn = normed(zh, 3)
                if emit_kv:
                    kb_ref[:, col - COL_KB:col - COL_KB + HEAD_DIM] = kn
                out = rotated(kn)
            else:
                if emit_kv:
                    vb_ref[:, col - COL_VB:col - COL_VB + HEAD_DIM] = zh
                if emit_vt:
                    r0 = (col - COL_VB) // HEAD_DIM * VT_ROWS
                    vt_ref[r0:r0 + HEAD_DIM, :] = zh.T.astype(BF16)
                    vt_ref[r0 + HEAD_DIM:r0 + VT_ROWS, :] = jnp.ones((BF16_ROWS, zh.shape[0]), BF16)
                out = zh
            z_ref[:, col:col + HEAD_DIM] = out.astype(BF16)
    for c0 in range(0, CZ_W, chunk):
        cz_ref[:, c0:c0 + chunk] = jnp.dot(hb, w_ref[:, QKV_W + c0:QKV_W + c0 + chunk],
                                           preferred_element_type=F32)


def _in_projection(x, mod, g1, w_in, layer, gains, rope_tabs, seq_len, tm):
    n, d = x.shape
    tm, tiles_per_seq = _token_tiling(n, seq_len, mod.shape[0], tm)
    mod_map = lambda i: (i // tiles_per_seq, 0, 0)
    const = lambda i: (0, 0)
    in_specs = [pl.BlockSpec((tm, d), lambda i: (i, 0)),
                pl.BlockSpec((1, N_MOD, d), mod_map),
                pl.BlockSpec((1, d), const),
                _resident_layer_spec(w_in, layer),
                pl.BlockSpec(gains.shape, const)]
    args = [x, mod, g1, w_in, gains]
    rope = rope_tabs is not None
    emit_kv, emit_vt = not rope, rope
    if rope:
        in_specs += [pl.BlockSpec((tm, HEAD_DIM), lambda i: (i % tiles_per_seq, 0))] * 2
        args += list(rope_tabs)
    out_specs = [pl.BlockSpec((tm, QKV_W), lambda i: (i, 0)),
                 pl.BlockSpec((tm, CZ_W), lambda i: (i, 0))]
    out_shape = [jax.ShapeDtypeStruct((n, QKV_W), BF16), jax.ShapeDtypeStruct((n, CZ_W), F32)]
    if emit_kv:
        out_specs += [pl.BlockSpec((tm, A_KV_W), lambda i: (i, 0))] * 4
        out_shape += [jax.ShapeDtypeStruct((n, A_KV_W), F32)] * 4
    if emit_vt:
        out_specs += [pl.BlockSpec((B_KV * VT_ROWS, tm), lambda i: (0, i))]
        out_shape += [jax.ShapeDtypeStruct((B_KV * VT_ROWS, n), BF16)]
    return pl.pallas_call(
        functools.partial(_inproj_kernel, rope=rope, emit_kv=emit_kv, emit_vt=emit_vt),
        grid=(n // tm,),
        in_specs=in_specs,
        out_specs=out_specs,
        out_shape=out_shape,
        compiler_params=_cparams(("arbitrary",), 56),
        name="in_projection_lat" if rope else "in_projection_ctx",
    )(*args)


def _softmax_pv(s_list, v_list, sink_col, exp=jnp.exp):
    m = s_list[0].max(axis=-1, keepdims=True)
    for s in s_list[1:]:
        m = jnp.maximum(m, s.max(axis=-1, keepdims=True))
    if sink_col is not None:
        m = jnp.maximum(m, sink_col)
    denom = None
    acc = None
    for s, v in zip(s_list, v_list):
        p = exp(s - m)
        ps = p.sum(axis=-1, keepdims=True)
        pv = jnp.dot(p.astype(BF16), v, preferred_element_type=F32)
        denom = ps if denom is None else denom + ps
        acc = pv if acc is None else acc + pv
    if sink_col is not None:
        denom = denom + exp(sink_col - m)
    return acc * (1.0 / denom)


def _stack_heads(ref, col0, n_heads):
    return jnp.concatenate([ref[:, col0 + g * HEAD_DIM:col0 + (g + 1) * HEAD_DIM]
                            for g in range(n_heads)], axis=0)


def _sink_column(sink_ref, head0, n_heads, rows_per_head):
    row = lax.broadcasted_iota(jnp.int32, (n_heads * rows_per_head, 1), 0)
    col = jnp.full((n_heads * rows_per_head, 1), sink_ref[head0], F32)
    for g in range(1, n_heads):
        col = jnp.where(row >= g * rows_per_head, sink_ref[head0 + g], col)
    return col


def _ctx_attn_kernel(sink_ref, z_ref, oa_ref, ob_ref):
    t = z_ref.shape[0]
    ga = A_HEADS // A_KV
    for kh in range(A_KV):
        q = _stack_heads(z_ref, kh * ga * HEAD_DIM, ga)
        k = z_ref[:, COL_KA + kh * HEAD_DIM:COL_KA + (kh + 1) * HEAD_DIM]
        v = z_ref[:, COL_VA + kh * HEAD_DIM:COL_VA + (kh + 1) * HEAD_DIM]
        o = _softmax_pv([_qk_t(q, k)], [v], _sink_column(sink_ref, kh * ga, ga, t))
        for g in range(ga):
            c = (kh * ga + g) * HEAD_DIM
            oa_ref[:, c:c + HEAD_DIM] = o[g * t:(g + 1) * t].astype(BF16)
    gb = B_HEADS // B_KV
    for kh in range(B_KV):
        q = _stack_heads(z_ref, COL_QB + kh * gb * HEAD_DIM, gb)
        k = z_ref[:, COL_KB + kh * HEAD_DIM:COL_KB + (kh + 1) * HEAD_DIM]
        v = z_ref[:, COL_VB + kh * HEAD_DIM:COL_VB + (kh + 1) * HEAD_DIM]
        o = _softmax_pv([_qk_t(q, k)], [v], None, exp=jnp.exp2)
        for g in range(gb):
            c = (kh * gb + g) * HEAD_DIM
            ob_ref[:, c:c + HEAD_DIM] = o[g * t:(g + 1) * t].astype(BF16)


def _ctx_attention(z, sink, seq_len):
    n = z.shape[0]
    return pl.pallas_call(
        _ctx_attn_kernel,
        grid=(n // seq_len,),
        in_specs=[pl.BlockSpec(memory_space=pltpu.SMEM),
                  pl.BlockSpec((seq_len, QKV_W), lambda b: (b, 0))],
        out_specs=[pl.BlockSpec((seq_len, A_Q_W), lambda b: (b, 0)),
                   pl.BlockSpec((seq_len, B_Q_W), lambda b: (b, 0))],
        out_shape=[jax.ShapeDtypeStruct((n, A_Q_W), BF16), jax.ShapeDtypeStruct((n, B_Q_W), BF16)],
        compiler_params=_cparams(("arbitrary",), 32),
        name="ctx_attention",
    )(sink, z)


def _win_attn_kernel(sink_ref, q_ref, kp_ref, kc_ref, kn_ref, vp_ref, vc_ref, vn_ref,
                     kx_ref, vx_ref, oa_ref, *, seq_len):
    tq = q_ref.shape[0]
    t0 = pl.program_id(1) * tq
    ga = A_HEADS // A_KV
    nk = tq + 2 * WINDOW
    row = lax.broadcasted_iota(jnp.int32, (ga * tq, nk), 0) & (tq - 1)
    col = lax.broadcasted_iota(jnp.int32, (ga * tq, nk), 1) - WINDOW
    key_pos = col + t0
    valid = (jnp.abs(col - row) <= WINDOW) & (key_pos >= 0) & (key_pos < seq_len)
    for kh in range(A_KV):
        hs = slice(kh * HEAD_DIM, (kh + 1) * HEAD_DIM)
        q = _stack_heads(q_ref, kh * ga * HEAD_DIM, ga)
        kw = jnp.concatenate([kp_ref[:, hs], kc_ref[:, hs], kn_ref[:, hs]], axis=0)
        vw = jnp.concatenate([vp_ref[:, hs], vc_ref[:, hs], vn_ref[:, hs]], axis=0)
        s_win = jnp.where(valid, _qk_t(q, kw), MASK_VALUE)
        s_ctx = _qk_t(q, kx_ref[0, :, hs])
        o = _softmax_pv([s_win, s_ctx], [vw, vx_ref[0, :, hs]],
                        _sink_column(sink_ref, kh * ga, ga, tq))
        for g in range(ga):
            c = (kh * ga + g) * HEAD_DIM
            oa_ref[:, c:c + HEAD_DIM] = o[g * tq:(g + 1) * tq].astype(BF16)


def _win_attention(z, kx, vx, sink, seq_len, tq):
    n = z.shape[0]
    nb = n // seq_len
    tq = min(tq, seq_len)
    assert tq & (tq - 1) == 0 and tq % WINDOW == 0
    nt = seq_len // tq
    halos_per_tile = tq // WINDOW
    n_halo = n // WINDOW
    past = kx.shape[1]
    cur = lambda cb: pl.BlockSpec((tq, A_KV_W), lambda b, t: (b * nt + t, cb))
    prev = lambda cb: pl.BlockSpec(
        (WINDOW, A_KV_W), lambda b, t: (jnp.maximum((b * nt + t) * halos_per_tile - 1, 0), cb))
    nxt = lambda cb: pl.BlockSpec(
        (WINDOW, A_KV_W),
        lambda b, t: (jnp.minimum((b * nt + t + 1) * halos_per_tile, n_halo - 1), cb))
    kcb = COL_KA // A_KV_W
    vcb = COL_VA // A_KV_W
    ctx = pl.BlockSpec((1, past, A_KV_W), lambda b, t: (b, 0, 0))
    return pl.pallas_call(
        functools.partial(_win_attn_kernel, seq_len=seq_len),
        grid=(nb, nt),
        in_specs=[pl.BlockSpec(memory_space=pltpu.SMEM),
                  pl.BlockSpec((tq, A_Q_W), lambda b, t: (b * nt + t, 0)),
                  prev(kcb), cur(kcb), nxt(kcb), prev(vcb), cur(vcb), nxt(vcb), ctx, ctx],
        out_specs=pl.BlockSpec((tq, A_Q_W), lambda b, t: (b * nt + t, 0)),
        out_shape=jax.ShapeDtypeStruct((n, A_Q_W), BF16),
        compiler_params=_cparams(("arbitrary", "arbitrary"), 40),
        name="win_attention",
    )(sink, z, z, z, z, z, z, z, kx, vx)


def _glob_attn_kernel(q_ref, kx_ref, vxt_ref, kl_ref, vlt_ref, ob_ref, *, tk):
    gb = B_HEADS // B_KV
    past = kx_ref.shape[1]
    assert past % tk == 0 and kl_ref.shape[0] % tk == 0
    blocks = [(r0, g) for r0 in range(0, q_ref.shape[0], ATTN_QBLK) for g in range(gb)]
    nq = len(blocks)
    queries = [q_ref[r0:r0 + ATTN_QBLK, g * HEAD_DIM:(g + 1) * HEAD_DIM] for r0, g in blocks]

    def update(st, vt, carry):
        c_max = st.max(axis=0, keepdims=True)
        if carry is None:
            m = c_max
            pt = jnp.exp2(st - m)
            return m, jnp.dot(vt, pt.astype(BF16), preferred_element_type=F32)
        m_old, acc = carry
        m = jnp.maximum(m_old, c_max)
        alpha = jnp.exp2(m_old - m)
        pt = jnp.exp2(st - m)
        acc = alpha * acc + jnp.dot(vt, pt.astype(BF16), preferred_element_type=F32)
        return m, acc

    def keys(i):
        c0 = i * tk
        if c0 < past:
            return kx_ref[0, c0:c0 + tk, :], vxt_ref[0, :, c0:c0 + tk]
        c0 -= past
        return kl_ref[c0:c0 + tk, :], vlt_ref[:, c0:c0 + tk]

    n_tiles = (past + kl_ref.shape[0]) // tk * nq
    score = lambda t: _qk_t(keys(t // nq)[0], queries[t % nq])
    carries = [None] * nq
    pending = [score(t) for t in range(min(ATTN_LOOKAHEAD, n_tiles))]
    for t in range(n_tiles):
        if t + ATTN_LOOKAHEAD < n_tiles:
            pending.append(score(t + ATTN_LOOKAHEAD))
        carries[t % nq] = update(pending.pop(0), keys(t // nq)[1], carries[t % nq])
    for (r0, g), (_, acc) in zip(blocks, carries):
        o = acc[:HEAD_DIM] * (1.0 / acc[HEAD_DIM:HEAD_DIM + 1])
        ob_ref[r0:r0 + ATTN_QBLK, g * HEAD_DIM:(g + 1) * HEAD_DIM] = o.T.astype(BF16)


def _glob_attention(z, vt, kx, vxt, seq_len, tq, tk):
    n = z.shape[0]
    nb = n // seq_len
    tq = min(tq, seq_len)
    tk = min(tk, seq_len)
    nt = seq_len // tq
    past = kx.shape[1]
    q_w = B_Q_W // B_KV
    return pl.pallas_call(
        functools.partial(_glob_attn_kernel, tk=tk),
        grid=(nb, B_KV, nt),
        in_specs=[pl.BlockSpec((tq, q_w), lambda b, kh, t: (b * nt + t, COL_QB // q_w + kh)),
                  pl.BlockSpec((1, past, HEAD_DIM), lambda b, kh, t: (b, 0, kh)),
                  pl.BlockSpec((1, VT_ROWS, past), lambda b, kh, t: (b, kh, 0)),
                  pl.BlockSpec((seq_len, HEAD_DIM), lambda b, kh, t: (b, COL_KB // HEAD_DIM + kh)),
                  pl.BlockSpec((VT_ROWS, seq_len), lambda b, kh, t: (kh, b))],
        out_specs=pl.BlockSpec((tq, q_w), lambda b, kh, t: (b * nt + t, kh)),
        out_shape=jax.ShapeDtypeStruct((n, B_Q_W), BF16),
        compiler_params=_cparams(("arbitrary", "arbitrary", "arbitrary"), 48),
        name="glob_attention",
    )(z, kx, vxt, z, vt)


def _conv_kernel(cp_ref, cc_ref, cn_ref, w_ref, b_ref, g_ref, beta_ref, o_ref, u_ref, *,
                 tiles_per_seq):
    tc = cc_ref.shape[0]
    ti = pl.program_id(0) % tiles_per_seq

    def glu(ref):
        return ref[:, :C_CH] * jax.nn.sigmoid(ref[:, C_CH:])

    u_ref[0:CONV_HALO, :] = jnp.where(ti > 0, glu(cp_ref), 0.0)
    u_ref[CONV_HALO:CONV_HALO + tc, :] = glu(cc_ref)
    u_ref[CONV_HALO + tc:, :] = jnp.where(ti < tiles_per_seq - 1, glu(cn_ref), 0.0)

    bias = b_ref[...]
    gamma = g_ref[...]
    beta = beta_ref[...]
    base = CONV_HALO - CONV_W // 2

    def body(r, carry):
        r0 = pl.multiple_of(r * CONV_ROWS, CONV_ROWS)
        y = None
        for b in range(SUBLANES):
            v = None
            for a in range((base + CONV_W - 1 - b) // SUBLANES + 1):
                k = SUBLANES * a + b - base
                if k < 0:
                    continue
                term = w_ref[k:k + 1, :] * u_ref[pl.ds(r0 + SUBLANES * a, CONV_ROWS + SUBLANES), :]
                v = term if v is None else v + term
            v = v[b:b + CONV_ROWS, :]
            y = v if y is None else y + v
        y = y + bias
        mu = jnp.mean(y, axis=-1, keepdims=True)
        yc = y - mu
        yn = yc * lax.rsqrt(jnp.mean(yc * yc, axis=-1, keepdims=True) + EPS) * gamma + beta
        o_ref[pl.ds(r0, CONV_ROWS), :] = jax.nn.silu(yn).astype(BF16)
        return carry

    lax.fori_loop(0, tc // CONV_ROWS, body, 0, unroll=CONV_UNROLL)


def _conformer_conv(cz, conv_w, conv_b, ln_g, ln_b, seq_len, tc):
    n = cz.shape[0]
    tc = min(tc, seq_len)
    tiles_per_seq = seq_len // tc
    halos_per_tile = tc // CONV_HALO
    n_halo = n // CONV_HALO
    const = lambda i: (0, 0)
    return pl.pallas_call(
        functools.partial(_conv_kernel, tiles_per_seq=tiles_per_seq),
        grid=(n // tc,),
        in_specs=[pl.BlockSpec((CONV_HALO, CZ_W),
                               lambda i: (jnp.maximum(i * halos_per_tile - 1, 0), 0)),
                  pl.BlockSpec((tc, CZ_W), lambda i: (i, 0)),
                  pl.BlockSpec((CONV_HALO, CZ_W),
                               lambda i: (jnp.minimum((i + 1) * halos_per_tile, n_halo - 1), 0)),
                  pl.BlockSpec((CONV_W, C_CH), const),
                  pl.BlockSpec((1, C_CH), const),
                  pl.BlockSpec((1, C_CH), const),
                  pl.BlockSpec((1, C_CH), const)],
        out_specs=pl.BlockSpec((tc, C_CH), lambda i: (i, 0)),
        out_shape=jax.ShapeDtypeStruct((n, C_CH), BF16),
        scratch_shapes=[pltpu.VMEM((tc + 2 * CONV_HALO, C_CH), F32)],
        compiler_params=_cparams(("arbitrary",), 32),
        name="conformer_conv",
    )(cz, cz, cz, conv_w, conv_b, ln_g, ln_b)


def _outproj_kernel(oa_ref, ob_ref, oc_ref, x_ref, mod_ref, g2_ref, w_ref, xo_ref, h2_ref):
    mix = jnp.concatenate([oa_ref[...], ob_ref[...], oc_ref[...]], axis=1)
    y = jnp.dot(mix, w_ref[...], preferred_element_type=F32)
    x = x_ref[...] + mod_ref[0, 2:3, :] * y
    xo_ref[...] = x
    h2 = _rms(x, g2_ref[...])
    h2_ref[...] = (h2 * (1.0 + mod_ref[0, 4:5, :]) + mod_ref[0, 3:4, :]).astype(BF16)


def _out_projection(oa, ob, oc, x, mod, g2, w_out, layer, seq_len, tm):
    n, d = x.shape
    tm, tiles_per_seq = _token_tiling(n, seq_len, mod.shape[0], tm)
    mod_map = lambda i: (i // tiles_per_seq, 0, 0)
    const = lambda i: (0, 0)
    row = lambda w: pl.BlockSpec((tm, w), lambda i: (i, 0))
    return pl.pallas_call(
        _outproj_kernel,
        grid=(n // tm,),
        in_specs=[row(A_Q_W), row(B_Q_W), row(C_CH), row(d),
                  pl.BlockSpec((1, N_MOD, d), mod_map),
                  pl.BlockSpec((1, d), const),
                  _resident_layer_spec(w_out, layer)],
        out_specs=[row(d), row(d)],
        out_shape=[jax.ShapeDtypeStruct((n, d), F32), jax.ShapeDtypeStruct((n, d), BF16)],
        compiler_params=_cparams(("arbitrary",), 56),
        name="out_projection",
    )(oa, ob, oc, x, mod, g2, w_out)


def _ffn_gate_kernel(h_ref, wgu_ref, a_ref):
    tf = a_ref.shape[1]
    gu = jnp.dot(h_ref[...], wgu_ref[...], preferred_element_type=F32)
    a_ref[...] = (jax.nn.silu(gu[:, :tf]) * gu[:, tf:]).astype(BF16)


def _ffn_down_kernel(a_ref, wd_ref, x_ref, mod_ref, o_ref):
    y = jnp.dot(a_ref[...], wd_ref[...], preferred_element_type=F32)
    o_ref[...] = x_ref[...] + mod_ref[0, 5:6, :] * y


def _ffn(h2, w_gate_up, w_down, layer, x, mod, seq_len, tm_gate, tm_down):
    n, d = x.shape
    d_ff = w_down.shape[1]
    tf = FFN_TILE
    tm_gate = min(tm_gate, n)
    a = pl.pallas_call(
        _ffn_gate_kernel,
        grid=(n // tm_gate, d_ff // tf),
        in_specs=[pl.BlockSpec((tm_gate, d), lambda i, f: (i, 0)),
                  pl.BlockSpec((None, d, 2 * tf), lambda i, f: (layer, 0, f))],
        out_specs=pl.BlockSpec((tm_gate, tf), lambda i, f: (i, f)),
        out_shape=jax.ShapeDtypeStruct((n, d_ff), BF16),
        compiler_params=_cparams(("arbitrary", "arbitrary"), 56),
        name="ffn_gate_up",
    )(h2, w_gate_up)
    tm, tiles_per_seq = _token_tiling(n, seq_len, mod.shape[0], tm_down)
    return pl.pallas_call(
        _ffn_down_kernel,
        grid=(n // tm,),
        in_specs=[pl.BlockSpec((tm, d_ff), lambda i: (i, 0)),
                  _resident_layer_spec(w_down, layer),
                  pl.BlockSpec((tm, d), lambda i: (i, 0)),
                  pl.BlockSpec((1, N_MOD, d), lambda i: (i // tiles_per_seq, 0, 0))],
        out_specs=pl.BlockSpec((tm, d), lambda i: (i, 0)),
        out_shape=jax.ShapeDtypeStruct((n, d), F32),
        compiler_params=_cparams(("arbitrary",), 56),
        name="ffn_down",
    )(a, w_down, x, mod)


def _rope_tables(n_tokens):
    pairs = HEAD_DIM // 4
    t = jnp.arange(n_tokens)
    row = (t // GRID_W).astype(F32)
    col = (t % GRID_W).astype(F32)
    inv = jnp.power(ROPE_BASE, -jnp.arange(pairs, dtype=F32) / pairs)
    ang = jnp.concatenate([row[:, None] * inv, col[:, None] * inv], axis=-1)
    cos = jnp.cos(ang)
    sin = jnp.sin(ang)
    return jnp.concatenate([cos, cos], axis=-1), jnp.concatenate([-sin, sin], axis=-1)


def _interleave_gate_up(w_gate, w_up):
    depth, d, d_ff = w_gate.shape
    assert d_ff % FFN_TILE == 0
    parts = []
    for c0 in range(0, d_ff, FFN_TILE):
        parts += [w_gate[..., c0:c0 + FFN_TILE], w_up[..., c0:c0 + FFN_TILE]]
    return jnp.concatenate(parts, axis=-1).astype(BF16)


def _values_t_with_ones(cache_v):
    b, depth, past, n_kv, d = cache_v.shape
    vt = jnp.transpose(cache_v.astype(BF16), (0, 1, 3, 4, 2))
    ones = jnp.ones((b, depth, n_kv, VT_ROWS - d, past), BF16)
    return jnp.concatenate([vt, ones], axis=3).reshape(b, depth, n_kv * VT_ROWS, past)


def _trunk_layer(x, mod, lw, layer, seq_len, rope_tabs=None, ctx_kv=None):
    (w_in, w_out, w_gate_up, w_down, g1, g2, gains, sink, conv_w, conv_b, cln_g, cln_b) = lw
    latent = ctx_kv is not None
    proj = _in_projection(x, mod, g1, w_in, layer, gains, rope_tabs, seq_len, tm=512)
    z, cz = proj[0], proj[1]
    if latent:
        ka_c, va_c, kb_c, vb_c = ctx_kv
        oa = _win_attention(z, ka_c, va_c, sink, seq_len, tq=512)
        ob = _glob_attention(z, proj[2], kb_c, vb_c, seq_len, tq=512, tk=256)
    else:
        oa, ob = _ctx_attention(z, sink, seq_len)
    oc = _conformer_conv(cz, conv_w, conv_b, cln_g, cln_b, seq_len, tc=512)
    x, h2 = _out_projection(oa, ob, oc, x, mod, g2, w_out, layer, seq_len, tm=512)
    x = _ffn(h2, w_gate_up, w_down, layer, x, mod, seq_len, tm_gate=1024, tm_down=512)
    return x, proj[2:]


def kernel(x_prompt, x_sample, cache_a_k, cache_a_v, cache_b_k, cache_b_v, c, c_ctx, w_ada, b_ada, w_in, w_out, w_gate, w_up, w_down, norm1_g, norm2_g, qnorm_a_g, knorm_a_g, qnorm_b_g, knorm_b_g, sink_a, conv_w, conv_b, conv_ln_g, conv_ln_b):
    batch, seq, d = x_prompt.shape
    dec_batch, dec_seq, _ = x_sample.shape
    depth = w_in.shape[0]
    past = cache_a_k.shape[2]

    n_rows = 1 + dec_batch
    pad_rows = -n_rows % 8
    cvecs = jnp.concatenate([c_ctx[None, :], c, jnp.zeros((pad_rows, d), F32)], axis=0)
    mods = _ada_mods(cvecs, w_ada, b_ada).reshape(depth, n_rows + pad_rows, N_MOD, d)

    rope_tabs = _rope_tables(dec_seq)
    w_in_b, w_out_b = w_in.astype(BF16), w_out.astype(BF16)
    w_gate_up_b, w_down_b = _interleave_gate_up(w_gate, w_up), w_down.astype(BF16)
    ctx_kv = [t.astype(BF16).reshape(dec_batch, depth, past, -1)
              for t in (cache_a_k, cache_a_v, cache_b_k, cache_b_v)]
    ctx_kv[3] = _values_t_with_ones(cache_b_v)

    y_prompt = x_prompt.reshape(batch * seq, d)
    y_sample = x_sample.reshape(dec_batch * dec_seq, d)
    new_kv = []
    for l in range(depth):
        gains = jnp.stack([qnorm_a_g[l], knorm_a_g[l], qnorm_b_g[l], knorm_b_g[l]])
        lw = (w_in_b, w_out_b, w_gate_up_b, w_down_b,
              norm1_g[l][None, :], norm2_g[l][None, :], gains, sink_a[l],
              conv_w[l], conv_b[l][None, :], conv_ln_g[l][None, :], conv_ln_b[l][None, :])
        y_prompt, kv = _trunk_layer(y_prompt, mods[l, 0:1], lw, l, seq)
        new_kv.append(kv)
        y_sample, _ = _trunk_layer(y_sample, mods[l, 1:n_rows], lw, l, dec_seq, rope_tabs,
                                   [t[:, l] for t in ctx_kv])

    def stacked(i, n_kv):
        return jnp.stack([kv[i].reshape(batch, seq, n_kv, HEAD_DIM) for kv in new_kv], axis=1)

    return (y_prompt.reshape(batch, seq, d), y_sample.reshape(dec_batch, dec_seq, d),
            stacked(0, A_KV), stacked(1, A_KV), stacked(2, B_KV), stacked(3, B_KV))
```

```python
import functools

import jax
import jax.numpy as jnp
from jax import lax
from jax.experimental import pallas as pl
from jax.experimental.pallas import tpu as pltpu

F32 = jnp.float32
BF16 = jnp.bfloat16

HEAD_DIM = 128
GRID_W = 64
A_HEADS = 4
A_KV = 2
B_HEADS = 8
B_KV = 2
C_CH = 512
CONV_W = 31
WINDOW = 128
ROPE_BASE = 10000.0
N_MOD = 6
EPS = 1e-6

A_Q_W = A_HEADS * HEAD_DIM
A_KV_W = A_KV * HEAD_DIM
B_Q_W = B_HEADS * HEAD_DIM
B_KV_W = B_KV * HEAD_DIM
QKV_W = A_Q_W + 2 * A_KV_W + B_Q_W + 2 * B_KV_W
CZ_W = 2 * C_CH
COL_KA = A_Q_W
COL_VA = COL_KA + A_KV_W
COL_QB = COL_VA + A_KV_W
COL_KB = COL_QB + B_Q_W
COL_VB = COL_KB + B_KV_W

V7X_VMEM_BYTES = 64 * 1024 * 1024
MASK_VALUE = -1e30
LOG2_E = 1.4426950408889634
CONV_HALO = 16
CONV_ROWS = 32
CONV_UNROLL = 4
WIN_LOOKAHEAD = 4
ATTN_QBLK = 256
ATTN_LOOKAHEAD = 14
SUBLANES = 8
BF16_ROWS = 16
VT_ROWS = HEAD_DIM + BF16_ROWS
FFN_TILE = 1408


def _cparams(semantics, vmem_mb):
    assert vmem_mb * 1024 * 1024 < V7X_VMEM_BYTES
    return pltpu.CompilerParams(dimension_semantics=semantics,
                                vmem_limit_bytes=vmem_mb * 1024 * 1024)


def _rms(x, g):
    return x * lax.rsqrt(jnp.mean(x * x, axis=-1, keepdims=True) + EPS) * g


def _qk_t(q, k):
    return lax.dot_general(q, k, (((1,), (1,)), ((), ())), preferred_element_type=F32)


def _resident_layer_spec(stack, layer):
    _, rows, cols = stack.shape
    return pl.BlockSpec((None, rows, cols), lambda *_: (layer, 0, 0), pipeline_mode=pl.Buffered(1))


def _token_tiling(n_tokens, seq_len, n_mod_rows, tm):
    span = n_tokens if n_mod_rows == 1 else seq_len
    tm = min(tm, span)
    assert span % tm == 0
    return tm, span // tm


def _mods_kernel(c_ref, w_ref, b_ref, o_ref):
    a = jax.nn.silu(c_ref[...]).astype(BF16)
    o_ref[0] = jnp.dot(a, w_ref[0].astype(BF16), preferred_element_type=F32) + b_ref[0]


def _ada_mods(cvecs, w_ada, b_ada):
    depth, d, n = w_ada.shape
    r = cvecs.shape[0]
    tn = 1024
    return pl.pallas_call(
        _mods_kernel,
        grid=(depth, n // tn),
        in_specs=[pl.BlockSpec((r, d), lambda l, j: (0, 0)),
                  pl.BlockSpec((1, d, tn), lambda l, j: (l, 0, j)),
                  pl.BlockSpec((1, 1, tn), lambda l, j: (l, 0, j))],
        out_specs=pl.BlockSpec((1, r, tn), lambda l, j: (l, 0, j)),
        out_shape=jax.ShapeDtypeStruct((depth, r, n), F32),
        compiler_params=_cparams(("arbitrary", "arbitrary"), 40),
        name="ada_mods",
    )(cvecs, w_ada, b_ada.reshape(depth, 1, n))


def _inproj_kernel(*refs, rope, emit_kv, emit_vt):
    x_ref, mod_ref, g1_ref, w_ref, gains_ref = refs[:5]
    refs = refs[5:]
    if rope:
        cos_ref, sin_ref = refs[:2]
        refs = refs[2:]
    z_ref, cz_ref = refs[:2]
    refs = refs[2:]
    if emit_kv:
        ka_ref, va_ref, kb_ref, vb_ref = refs[:4]
        refs = refs[4:]
    if emit_vt:
        vt_ref, = refs

    h = _rms(x_ref[...], g1_ref[...])
    h = h * (1.0 + mod_ref[0, 1:2, :]) + mod_ref[0, 0:1, :]
    hb = h.astype(BF16)
    if rope:
        cos = cos_ref[...]
        sin = sin_ref[...]
    q_scale = HEAD_DIM ** -0.5
    q_scale_b = q_scale * LOG2_E

    def normed(zh, gain_idx):
        return _rms(zh, gains_ref[gain_idx:gain_idx + 1, :])

    def store_vt(kv_head, zh):
        r0 = kv_head * VT_ROWS
        vt_ref[r0:r0 + HEAD_DIM, :] = zh.T.astype(BF16)
        vt_ref[r0 + HEAD_DIM:r0 + VT_ROWS, :] = jnp.ones((BF16_ROWS, zh.shape[0]), BF16)

    def rotated(zh):
        if not rope:
            return zh
        return zh * cos + pltpu.roll(zh, HEAD_DIM // 2, axis=1) * sin

    chunk = 512
    for c0 in range(0, QKV_W, chunk):
        zc = jnp.dot(hb, w_ref[:, c0:c0 + chunk], preferred_element_type=F32)
        for j in range(chunk // HEAD_DIM):
            col = c0 + j * HEAD_DIM
            zh = zc[:, j * HEAD_DIM:(j + 1) * HEAD_DIM]
            if col < COL_KA:
                out = rotated(normed(zh, 0)) * q_scale
            elif col < COL_VA:
                kn = normed(zh, 1)
                if emit_kv:
                    ka_ref[:, col - COL_KA:col - COL_KA + HEAD_DIM] = kn
                out = rotated(kn)
            elif col < COL_QB:
                if emit_kv:
                    va_ref[:, col - COL_VA:col - COL_VA + HEAD_DIM] = zh
                if emit_vt:
                    store_vt((col - COL_VA) // HEAD_DIM, zh)
                out = zh
            elif col < COL_KB:
                out = rotated(normed(zh, 2)) * q_scale_b
            elif col < COL_VB:
                kn = normed(zh, 3)
                if emit_kv:
                    kb_ref[:, col - COL_KB:col - COL_KB + HEAD_DIM] = kn
                out = rotated(kn)
            else:
                if emit_kv:
                    vb_ref[:, col - COL_VB:col - COL_VB + HEAD_DIM] = zh
                if emit_vt:
                    store_vt(A_KV + (col - COL_VB) // HEAD_DIM, zh)
                out = zh
            z_ref[:, col:col + HEAD_DIM] = out.astype(BF16)
    for c0 in range(0, CZ_W, chunk):
        cz_ref[:, c0:c0 + chunk] = jnp.dot(hb, w_ref[:, QKV_W + c0:QKV_W + c0 + chunk],
                                           preferred_element_type=F32)


def _in_projection(x, mod, g1, w_in, layer, gains, rope_tabs, seq_len, tm):
    n, d = x.shape
    tm, tiles_per_seq = _token_tiling(n, seq_len, mod.shape[0], tm)
    mod_map = lambda i: (i // tiles_per_seq, 0, 0)
    const = lambda i: (0, 0)
    in_specs = [pl.BlockSpec((tm, d), lambda i: (i, 0)),
                pl.BlockSpec((1, N_MOD, d), mod_map),
                pl.BlockSpec((1, d), const),
                _resident_layer_spec(w_in, layer),
                pl.BlockSpec(gains.shape, const)]
    args = [x, mod, g1, w_in, gains]
    rope = rope_tabs is not None
    emit_kv, emit_vt = not rope, rope
    if rope:
        in_specs += [pl.BlockSpec((tm, HEAD_DIM), lambda i: (i % tiles_per_seq, 0))] * 2
        args += list(rope_tabs)
    out_specs = [pl.BlockSpec((tm, QKV_W), lambda i: (i, 0)),
                 pl.BlockSpec((tm, CZ_W), lambda i: (i, 0))]
    out_shape = [jax.ShapeDtypeStruct((n, QKV_W), BF16), jax.ShapeDtypeStruct((n, CZ_W), F32)]
    if emit_kv:
        out_specs += [pl.BlockSpec((tm, A_KV_W), lambda i: (i, 0))] * 4
        out_shape += [jax.ShapeDtypeStruct((n, A_KV_W), F32)] * 4
    if emit_vt:
        out_specs += [pl.BlockSpec(((A_KV + B_KV) * VT_ROWS, tm), lambda i: (0, i))]
        out_shape += [jax.ShapeDtypeStruct(((A_KV + B_KV) * VT_ROWS, n), BF16)]
    return pl.pallas_call(
        functools.partial(_inproj_kernel, rope=rope, emit_kv=emit_kv, emit_vt=emit_vt),
        grid=(n // tm,),
        in_specs=in_specs,
        out_specs=out_specs,
        out_shape=out_shape,
        compiler_params=_cparams(("arbitrary",), 56),
        name="in_projection_lat" if rope else "in_projection_ctx",
    )(*args)


def _softmax_pv(s_list, v_list, sink_col, exp=jnp.exp):
    m = s_list[0].max(axis=-1, keepdims=True)
    for s in s_list[1:]:
        m = jnp.maximum(m, s.max(axis=-1, keepdims=True))
    if sink_col is not None:
        m = jnp.maximum(m, sink_col)
    denom = None
    acc = None
    for s, v in zip(s_list, v_list):
        p = exp(s - m)
        ps = p.sum(axis=-1, keepdims=True)
        pv = jnp.dot(p.astype(BF16), v, preferred_element_type=F32)
        denom = ps if denom is None else denom + ps
        acc = pv if acc is None else acc + pv
    if sink_col is not None:
        denom = denom + exp(sink_col - m)
    return acc * (1.0 / denom)


def _stack_heads(ref, col0, n_heads):
    return jnp.concatenate([ref[:, col0 + g * HEAD_DIM:col0 + (g + 1) * HEAD_DIM]
                            for g in range(n_heads)], axis=0)


def _sink_column(sink_ref, head0, n_heads, rows_per_head):
    row = lax.broadcasted_iota(jnp.int32, (n_heads * rows_per_head, 1), 0)
    col = jnp.full((n_heads * rows_per_head, 1), sink_ref[head0], F32)
    for g in range(1, n_heads):
        col = jnp.where(row >= g * rows_per_head, sink_ref[head0 + g], col)
    return col


def _ctx_attn_kernel(sink_ref, z_ref, oa_ref, ob_ref):
    t = z_ref.shape[0]
    ga = A_HEADS // A_KV
    for kh in range(A_KV):
        q = _stack_heads(z_ref, kh * ga * HEAD_DIM, ga)
        k = z_ref[:, COL_KA + kh * HEAD_DIM:COL_KA + (kh + 1) * HEAD_DIM]
        v = z_ref[:, COL_VA + kh * HEAD_DIM:COL_VA + (kh + 1) * HEAD_DIM]
        o = _softmax_pv([_qk_t(q, k)], [v], _sink_column(sink_ref, kh * ga, ga, t))
        for g in range(ga):
            c = (kh * ga + g) * HEAD_DIM
            oa_ref[:, c:c + HEAD_DIM] = o[g * t:(g + 1) * t].astype(BF16)
    gb = B_HEADS // B_KV
    for kh in range(B_KV):
        q = _stack_heads(z_ref, COL_QB + kh * gb * HEAD_DIM, gb)
        k = z_ref[:, COL_KB + kh * HEAD_DIM:COL_KB + (kh + 1) * HEAD_DIM]
        v = z_ref[:, COL_VB + kh * HEAD_DIM:COL_VB + (kh + 1) * HEAD_DIM]
        o = _softmax_pv([_qk_t(q, k)], [v], None, exp=jnp.exp2)
        for g in range(gb):
            c = (kh * gb + g) * HEAD_DIM
            ob_ref[:, c:c + HEAD_DIM] = o[g * t:(g + 1) * t].astype(BF16)


def _ctx_attention(z, sink, seq_len):
    n = z.shape[0]
    return pl.pallas_call(
        _ctx_attn_kernel,
        grid=(n // seq_len,),
        in_specs=[pl.BlockSpec(memory_space=pltpu.SMEM),
                  pl.BlockSpec((seq_len, QKV_W), lambda b: (b, 0))],
        out_specs=[pl.BlockSpec((seq_len, A_Q_W), lambda b: (b, 0)),
                   pl.BlockSpec((seq_len, B_Q_W), lambda b: (b, 0))],
        out_shape=[jax.ShapeDtypeStruct((n, A_Q_W), BF16), jax.ShapeDtypeStruct((n, B_Q_W), BF16)],
        compiler_params=_cparams(("arbitrary",), 32),
        name="ctx_attention",
    )(sink, z)


def _win_attn_kernel(sink_ref, q_ref, kp_ref, kc_ref, kn_ref, vtp_ref, vtc_ref, vtn_ref,
                     kx_ref, vxt_ref, oa_ref, *, seq_len):
    tq = q_ref.shape[0]
    t0 = pl.program_id(1) * tq
    ga = A_HEADS // A_KV
    nsub = tq // WINDOW
    w = WINDOW
    key = lax.broadcasted_iota(jnp.int32, (w, ga * w), 0)
    qry = lax.broadcasted_iota(jnp.int32, (w, ga * w), 1) & (w - 1)
    lane = lax.broadcasted_iota(jnp.int32, (1, ga * w), 1)
    prev_mask = key >= qry
    next_mask = key <= qry
    tiles = [(kh, i) for kh in range(A_KV) for i in range(nsub)]

    def rows(i, prev_ref, cur_ref, next_ref, hs, axis):
        def cur(j0, j1):
            sl = slice(j0 * w, j1 * w)
            return cur_ref[sl, hs] if axis == 0 else cur_ref[hs, sl]
        halo = lambda ref: ref[:, hs] if axis == 0 else ref[hs, :]
        lo, hi = max(i - 1, 0), min(i + 2, nsub)
        parts = ([halo(prev_ref)] if i == 0 else []) + [cur(lo, hi)] + (
            [halo(next_ref)] if i == nsub - 1 else [])
        return parts[0] if len(parts) == 1 else jnp.concatenate(parts, axis=axis)

    def scores(kh, i):
        hs = slice(kh * HEAD_DIM, (kh + 1) * HEAD_DIM)
        q = jnp.concatenate([q_ref[i * w:(i + 1) * w, (kh * ga + g) * HEAD_DIM:(kh * ga + g + 1) * HEAD_DIM]
                             for g in range(ga)], axis=0)
        return _qk_t(rows(i, kp_ref, kc_ref, kn_ref, hs, 0), q), _qk_t(kx_ref[0, :, hs], q)

    def finish(kh, i, st_win, st_ctx):
        vs = slice(kh * VT_ROWS, (kh + 1) * VT_ROWS)
        prev_ok = prev_mask if i > 0 else prev_mask & (t0 > 0)
        next_ok = next_mask if i < nsub - 1 else next_mask & (t0 + tq < seq_len)
        parts = [jnp.where(prev_ok, st_win[0:w], MASK_VALUE), st_win[w:2 * w],
                 jnp.where(next_ok, st_win[2 * w:3 * w], MASK_VALUE)]
        sink = jnp.full((1, ga * w), sink_ref[kh * ga], F32)
        for g in range(1, ga):
            sink = jnp.where(lane >= g * w, sink_ref[kh * ga + g], sink)
        m = jnp.maximum(sink, st_ctx.max(axis=0, keepdims=True))
        for s in parts:
            m = jnp.maximum(m, s.max(axis=0, keepdims=True))
        pt_win = jnp.concatenate([jnp.exp(s - m) for s in parts], axis=0).astype(BF16)
        pt_ctx = jnp.exp(st_ctx - m).astype(BF16)
        acc = jnp.dot(rows(i, vtp_ref, vtc_ref, vtn_ref, vs, 1), pt_win, preferred_element_type=F32)
        acc = acc + jnp.dot(vxt_ref[0, vs, :], pt_ctx, preferred_element_type=F32)
        denom = acc[HEAD_DIM:HEAD_DIM + 1] + jnp.exp(sink - m)
        o = (acc[:HEAD_DIM] * (1.0 / denom)).T
        for g in range(ga):
            c = (kh * ga + g) * HEAD_DIM
            oa_ref[i * w:(i + 1) * w, c:c + HEAD_DIM] = o[g * w:(g + 1) * w].astype(BF16)

    pending = [scores(*tiles[t]) for t in range(min(WIN_LOOKAHEAD, len(tiles)))]
    for t, (kh, i) in enumerate(tiles):
        if t + WIN_LOOKAHEAD < len(tiles):
            pending.append(scores(*tiles[t + WIN_LOOKAHEAD]))
        finish(kh, i, *pending.pop(0))


def _win_attention(z, vt, kx, vxt, sink, seq_len, tq):
    n = z.shape[0]
    nb = n // seq_len
    tq = min(tq, seq_len)
    assert tq % WINDOW == 0
    nt = seq_len // tq
    halos_per_tile = tq // WINDOW
    n_halo = n // WINDOW
    past = kx.shape[1]
    vt_rows = A_KV * VT_ROWS
    tile = lambda b, t: b * nt + t
    prev_halo = lambda b, t: jnp.maximum(tile(b, t) * halos_per_tile - 1, 0)
    next_halo = lambda b, t: jnp.minimum((tile(b, t) + 1) * halos_per_tile, n_halo - 1)
    kcb = COL_KA // A_KV_W
    return pl.pallas_call(
        functools.partial(_win_attn_kernel, seq_len=seq_len),
        grid=(nb, nt),
        in_specs=[pl.BlockSpec(memory_space=pltpu.SMEM),
                  pl.BlockSpec((tq, A_Q_W), lambda b, t: (tile(b, t), 0)),
                  pl.BlockSpec((WINDOW, A_KV_W), lambda b, t: (prev_halo(b, t), kcb)),
                  pl.BlockSpec((tq, A_KV_W), lambda b, t: (tile(b, t), kcb)),
                  pl.BlockSpec((WINDOW, A_KV_W), lambda b, t: (next_halo(b, t), kcb)),
                  pl.BlockSpec((vt_rows, WINDOW), lambda b, t: (0, prev_halo(b, t))),
                  pl.BlockSpec((vt_rows, tq), lambda b, t: (0, tile(b, t))),
                  pl.BlockSpec((vt_rows, WINDOW), lambda b, t: (0, next_halo(b, t))),
                  pl.BlockSpec((1, past, A_KV_W), lambda b, t: (b, 0, 0)),
                  pl.BlockSpec((1, vt_rows, past), lambda b, t: (b, 0, 0))],
        out_specs=pl.BlockSpec((tq, A_Q_W), lambda b, t: (tile(b, t), 0)),
        out_shape=jax.ShapeDtypeStruct((n, A_Q_W), BF16),
        compiler_params=_cparams(("arbitrary", "arbitrary"), 40),
        name="win_attention",
    )(sink, z, z, z, z, vt, vt, vt, kx, vxt)


def _glob_attn_kernel(q_ref, kx_ref, vxt_ref, kl_ref, vlt_ref, ob_ref, *, tk):
    gb = B_HEADS // B_KV
    past = kx_ref.shape[1]
    assert past % tk == 0 and kl_ref.shape[0] % tk == 0
    blocks = [(r0, g) for r0 in range(0, q_ref.shape[0], ATTN_QBLK) for g in range(gb)]
    nq = len(blocks)
    queries = [q_ref[r0:r0 + ATTN_QBLK, g * HEAD_DIM:(g + 1) * HEAD_DIM] for r0, g in blocks]

    def update(st, vt, carry):
        c_max = st.max(axis=0, keepdims=True)
        if carry is None:
            m = c_max
            pt = jnp.exp2(st - m)
            return m, jnp.dot(vt, pt.astype(BF16), preferred_element_type=F32)
        m_old, acc = carry
        m = jnp.maximum(m_old, c_max)
        alpha = jnp.exp2(m_old - m)
        pt = jnp.exp2(st - m)
        acc = alpha * acc + jnp.dot(vt, pt.astype(BF16), preferred_element_type=F32)
        return m, acc

    def keys(i):
        c0 = i * tk
        if c0 < past:
            return kx_ref[0, c0:c0 + tk, :], vxt_ref[0, :, c0:c0 + tk]
        c0 -= past
        return kl_ref[c0:c0 + tk, :], vlt_ref[:, c0:c0 + tk]

    n_tiles = (past + kl_ref.shape[0]) // tk * nq
    score = lambda t: _qk_t(keys(t // nq)[0], queries[t % nq])
    carries = [None] * nq
    pending = [score(t) for t in range(min(ATTN_LOOKAHEAD, n_tiles))]
    for t in range(n_tiles):
        if t + ATTN_LOOKAHEAD < n_tiles:
            pending.append(score(t + ATTN_LOOKAHEAD))
        carries[t % nq] = update(pending.pop(0), keys(t // nq)[1], carries[t % nq])
    for (r0, g), (_, acc) in zip(blocks, carries):
        o = acc[:HEAD_DIM] * (1.0 / acc[HEAD_DIM:HEAD_DIM + 1])
        ob_ref[r0:r0 + ATTN_QBLK, g * HEAD_DIM:(g + 1) * HEAD_DIM] = o.T.astype(BF16)


def _glob_attention(z, vt, kx, vxt, seq_len, tq, tk):
    n = z.shape[0]
    nb = n // seq_len
    tq = min(tq, seq_len)
    tk = min(tk, seq_len)
    nt = seq_len // tq
    past = kx.shape[1]
    q_w = B_Q_W // B_KV
    return pl.pallas_call(
        functools.partial(_glob_attn_kernel, tk=tk),
        grid=(nb, B_KV, nt),
        in_specs=[pl.BlockSpec((tq, q_w), lambda b, kh, t: (b * nt + t, COL_QB // q_w + kh)),
                  pl.BlockSpec((1, past, HEAD_DIM), lambda b, kh, t: (b, 0, kh)),
                  pl.BlockSpec((1, VT_ROWS, past), lambda b, kh, t: (b, kh, 0)),
                  pl.BlockSpec((seq_len, HEAD_DIM), lambda b, kh, t: (b, COL_KB // HEAD_DIM + kh)),
                  pl.BlockSpec((VT_ROWS, seq_len), lambda b, kh, t: (A_KV + kh, b))],
        out_specs=pl.BlockSpec((tq, q_w), lambda b, kh, t: (b * nt + t, kh)),
        out_shape=jax.ShapeDtypeStruct((n, B_Q_W), BF16),
        compiler_params=_cparams(("arbitrary", "arbitrary", "arbitrary"), 48),
        name="glob_attention",
    )(z, kx, vxt, z, vt)


def _conv_kernel(cp_ref, cc_ref, cn_ref, w_ref, b_ref, g_ref, beta_ref, o_ref, u_ref, *,
                 tiles_per_seq):
    tc = cc_ref.shape[0]
    ti = pl.program_id(0) % tiles_per_seq

    def glu(ref):
        return ref[:, :C_CH] * jax.nn.sigmoid(ref[:, C_CH:])

    u_ref[0:CONV_HALO, :] = jnp.where(ti > 0, glu(cp_ref), 0.0)
    u_ref[CONV_HALO:CONV_HALO + tc, :] = glu(cc_ref)
    u_ref[CONV_HALO + tc:, :] = jnp.where(ti < tiles_per_seq - 1, glu(cn_ref), 0.0)

    bias = b_ref[...]
    gamma = g_ref[...]
    beta = beta_ref[...]
    base = CONV_HALO - CONV_W // 2

    def body(r, carry):
        r0 = pl.multiple_of(r * CONV_ROWS, CONV_ROWS)
        y = None
        for b in range(SUBLANES):
            v = None
            for a in range((base + CONV_W - 1 - b) // SUBLANES + 1):
                k = SUBLANES * a + b - base
                if k < 0:
                    continue
                term = w_ref[k:k + 1, :] * u_ref[pl.ds(r0 + SUBLANES * a, CONV_ROWS + SUBLANES), :]
                v = term if v is None else v + term
            v = v[b:b + CONV_ROWS, :]
            y = v if y is None else y + v
        y = y + bias
        mu = jnp.mean(y, axis=-1, keepdims=True)
        yc = y - mu
        yn = yc * lax.rsqrt(jnp.mean(yc * yc, axis=-1, keepdims=True) + EPS) * gamma + beta
        o_ref[pl.ds(r0, CONV_ROWS), :] = jax.nn.silu(yn).astype(BF16)
        return carry

    lax.fori_loop(0, tc // CONV_ROWS, body, 0, unroll=CONV_UNROLL)


def _conformer_conv(cz, conv_w, conv_b, ln_g, ln_b, seq_len, tc):
    n = cz.shape[0]
    tc = min(tc, seq_len)
    tiles_per_seq = seq_len // tc
    halos_per_tile = tc // CONV_HALO
    n_halo = n // CONV_HALO
    const = lambda i: (0, 0)
    return pl.pallas_call(
        functools.partial(_conv_kernel, tiles_per_seq=tiles_per_seq),
        grid=(n // tc,),
        in_specs=[pl.BlockSpec((CONV_HALO, CZ_W),
                               lambda i: (jnp.maximum(i * halos_per_tile - 1, 0), 0)),
                  pl.BlockSpec((tc, CZ_W), lambda i: (i, 0)),
                  pl.BlockSpec((CONV_HALO, CZ_W),
                               lambda i: (jnp.minimum((i + 1) * halos_per_tile, n_halo - 1), 0)),
                  pl.BlockSpec((CONV_W, C_CH), const),
                  pl.BlockSpec((1, C_CH), const),
                  pl.BlockSpec((1, C_CH), const),
                  pl.BlockSpec((1, C_CH), const)],
        out_specs=pl.BlockSpec((tc, C_CH), lambda i: (i, 0)),
        out_shape=jax.ShapeDtypeStruct((n, C_CH), BF16),
        scratch_shapes=[pltpu.VMEM((tc + 2 * CONV_HALO, C_CH), F32)],
        compiler_params=_cparams(("arbitrary",), 32),
        name="conformer_conv",
    )(cz, cz, cz, conv_w, conv_b, ln_g, ln_b)


def _outproj_kernel(oa_ref, ob_ref, oc_ref, x_ref, mod_ref, g2_ref, w_ref, xo_ref, h2_ref):
    mix = jnp.concatenate([oa_ref[...], ob_ref[...], oc_ref[...]], axis=1)
    y = jnp.dot(mix, w_ref[...], preferred_element_type=F32)
    x = x_ref[...] + mod_ref[0, 2:3, :] * y
    xo_ref[...] = x
    h2 = _rms(x, g2_ref[...])
    h2_ref[...] = (h2 * (1.0 + mod_ref[0, 4:5, :]) + mod_ref[0, 3:4, :]).astype(BF16)


def _out_projection(oa, ob, oc, x, mod, g2, w_out, layer, seq_len, tm):
    n, d = x.shape
    tm, tiles_per_seq = _token_tiling(n, seq_len, mod.shape[0], tm)
    mod_map = lambda i: (i // tiles_per_seq, 0, 0)
    const = lambda i: (0, 0)
    row = lambda w: pl.BlockSpec((tm, w), lambda i: (i, 0))
    return pl.pallas_call(
        _outproj_kernel,
        grid=(n // tm,),
        in_specs=[row(A_Q_W), row(B_Q_W), row(C_CH), row(d),
                  pl.BlockSpec((1, N_MOD, d), mod_map),
                  pl.BlockSpec((1, d), const),
                  _resident_layer_spec(w_out, layer)],
        out_specs=[row(d), row(d)],
        out_shape=[jax.ShapeDtypeStruct((n, d), F32), jax.ShapeDtypeStruct((n, d), BF16)],
        compiler_params=_cparams(("arbitrary",), 56),
        name="out_projection",
    )(oa, ob, oc, x, mod, g2, w_out)


def _ffn_gate_kernel(h_ref, wgu_ref, a_ref):
    tf = a_ref.shape[1]
    gu = jnp.dot(h_ref[...], wgu_ref[...], preferred_element_type=F32)
    a_ref[...] = (jax.nn.silu(gu[:, :tf]) * gu[:, tf:]).astype(BF16)


def _ffn_down_kernel(a_ref, wd_ref, x_ref, mod_ref, o_ref):
    y = jnp.dot(a_ref[...], wd_ref[...], preferred_element_type=F32)
    o_ref[...] = x_ref[...] + mod_ref[0, 5:6, :] * y


def _ffn(h2, w_gate_up, w_down, layer, x, mod, seq_len, tm_gate, tm_down):
    n, d = x.shape
    d_ff = w_down.shape[1]
    tf = FFN_TILE
    tm_gate = min(tm_gate, n)
    a = pl.pallas_call(
        _ffn_gate_kernel,
        grid=(n // tm_gate, d_ff // tf),
        in_specs=[pl.BlockSpec((tm_gate, d), lambda i, f: (i, 0)),
                  pl.BlockSpec((None, d, 2 * tf), lambda i, f: (layer, 0, f))],
        out_specs=pl.BlockSpec((tm_gate, tf), lambda i, f: (i, f)),
        out_shape=jax.ShapeDtypeStruct((n, d_ff), BF16),
        compiler_params=_cparams(("arbitrary", "arbitrary"), 56),
        name="ffn_gate_up",
    )(h2, w_gate_up)
    tm, tiles_per_seq = _token_tiling(n, seq_len, mod.shape[0], tm_down)
    return pl.pallas_call(
        _ffn_down_kernel,
        grid=(n // tm,),
        in_specs=[pl.BlockSpec((tm, d_ff), lambda i: (i, 0)),
                  _resident_layer_spec(w_down, layer),
                  pl.BlockSpec((tm, d), lambda i: (i, 0)),
                  pl.BlockSpec((1, N_MOD, d), lambda i: (i // tiles_per_seq, 0, 0))],
        out_specs=pl.BlockSpec((tm, d), lambda i: (i, 0)),
        out_shape=jax.ShapeDtypeStruct((n, d), F32),
        compiler_params=_cparams(("arbitrary",), 56),
        name="ffn_down",
    )(a, w_down, x, mod)


def _rope_tables(n_tokens):
    pairs = HEAD_DIM // 4
    t = jnp.arange(n_tokens)
    row = (t // GRID_W).astype(F32)
    col = (t % GRID_W).astype(F32)
    inv = jnp.power(ROPE_BASE, -jnp.arange(pairs, dtype=F32) / pairs)
    ang = jnp.concatenate([row[:, None] * inv, col[:, None] * inv], axis=-1)
    cos = jnp.cos(ang)
    sin = jnp.sin(ang)
    return jnp.concatenate([cos, cos], axis=-1), jnp.concatenate([-sin, sin], axis=-1)


def _interleave_gate_up(w_gate, w_up):
    depth, d, d_ff = w_gate.shape
    assert d_ff % FFN_TILE == 0
    parts = []
    for c0 in range(0, d_ff, FFN_TILE):
        parts += [w_gate[..., c0:c0 + FFN_TILE], w_up[..., c0:c0 + FFN_TILE]]
    return jnp.concatenate(parts, axis=-1).astype(BF16)


def _values_t_with_ones(cache_v):
    b, depth, past, n_kv, d = cache_v.shape
    vt = jnp.transpose(cache_v.astype(BF16), (0, 1, 3, 4, 2))
    ones = jnp.ones((b, depth, n_kv, VT_ROWS - d, past), BF16)
    return jnp.concatenate([vt, ones], axis=3).reshape(b, depth, n_kv * VT_ROWS, past)


def _trunk_layer(x, mod, lw, layer, seq_len, rope_tabs=None, ctx_kv=None):
    (w_in, w_out, w_gate_up, w_down, g1, g2, gains, sink, conv_w, conv_b, cln_g, cln_b) = lw
    latent = ctx_kv is not None
    proj = _in_projection(x, mod, g1, w_in, layer, gains, rope_tabs, seq_len, tm=512)
    z, cz = proj[0], proj[1]
    if latent:
        ka_c, va_c, kb_c, vb_c = ctx_kv
        oa = _win_attention(z, proj[2], ka_c, va_c, sink, seq_len, tq=512)
        ob = _glob_attention(z, proj[2], kb_c, vb_c, seq_len, tq=512, tk=256)
    else:
        oa, ob = _ctx_attention(z, sink, seq_len)
    oc = _conformer_conv(cz, conv_w, conv_b, cln_g, cln_b, seq_len, tc=512)
    x, h2 = _out_projection(oa, ob, oc, x, mod, g2, w_out, layer, seq_len, tm=512)
    x = _ffn(h2, w_gate_up, w_down, layer, x, mod, seq_len, tm_gate=1024, tm_down=512)
    return x, proj[2:]


def kernel(x_prompt, x_sample, cache_a_k, cache_a_v, cache_b_k, cache_b_v, c, c_ctx, w_ada, b_ada, w_in, w_out, w_gate, w_up, w_down, norm1_g, norm2_g, qnorm_a_g, knorm_a_g, qnorm_b_g, knorm_b_g, sink_a, conv_w, conv_b, conv_ln_g, conv_ln_b):
    batch, seq, d = x_prompt.shape
    dec_batch, dec_seq, _ = x_sample.shape
    depth = w_in.shape[0]
    past = cache_a_k.shape[2]

    n_rows = 1 + dec_batch
    pad_rows = -n_rows % 8
    cvecs = jnp.concatenate([c_ctx[None, :], c, jnp.zeros((pad_rows, d), F32)], axis=0)
    mods = _ada_mods(cvecs, w_ada, b_ada).reshape(depth, n_rows + pad_rows, N_MOD, d)

    rope_tabs = _rope_tables(dec_seq)
    w_in_b, w_out_b = w_in.astype(BF16), w_out.astype(BF16)
    w_gate_up_b, w_down_b = _interleave_gate_up(w_gate, w_up), w_down.astype(BF16)
    ctx_kv = [t.astype(BF16).reshape(dec_batch, depth, past, -1)
              for t in (cache_a_k, cache_a_v, cache_b_k, cache_b_v)]
    ctx_kv[1], ctx_kv[3] = _values_t_with_ones(cache_a_v), _values_t_with_ones(cache_b_v)

    y_prompt = x_prompt.reshape(batch * seq, d)
    y_sample = x_sample.reshape(dec_batch * dec_seq, d)
    new_kv = []
    for l in range(depth):
        gains = jnp.stack([qnorm_a_g[l], knorm_a_g[l], qnorm_b_g[l], knorm_b_g[l]])
        lw = (w_in_b, w_out_b, w_gate_up_b, w_down_b,
              norm1_g[l][None, :], norm2_g[l][None, :], gains, sink_a[l],
              conv_w[l], conv_b[l][None, :], conv_ln_g[l][None, :], conv_ln_b[l][None, :])
        y_prompt, kv = _trunk_layer(y_prompt, mods[l, 0:1], lw, l, seq)
        new_kv.append(kv)
        y_sample, _ = _trunk_layer(y_sample, mods[l, 1:n_rows], lw, l, dec_seq, rope_tabs,
                                   [t[:, l] for t in ctx_kv])

    def stacked(i, n_kv):
        return jnp.stack([kv[i].reshape(batch, seq, n_kv, HEAD_DIM) for kv in new_kv], axis=1)

    return (y_prompt.reshape(batch, seq, d), y_sample.reshape(dec_batch, dec_seq, d),
            stacked(0, A_KV), stacked(1, A_KV), stacked(2, B_KV), stacked(3, B_KV))
```

```python
import functools

import jax
import jax.numpy as jnp
from jax import lax
from jax.experimental import pallas as pl
from jax.experimental.pallas import tpu as pltpu

F32 = jnp.float32
BF16 = jnp.bfloat16

HEAD_DIM = 128
GRID_W = 64
A_HEADS = 4
A_KV = 2
B_HEADS = 8
B_KV = 2
C_CH = 512
CONV_W = 31
WINDOW = 128
ROPE_BASE = 10000.0
N_MOD = 6
EPS = 1e-6

A_Q_W = A_HEADS * HEAD_DIM
A_KV_W = A_KV * HEAD_DIM
B_Q_W = B_HEADS * HEAD_DIM
B_KV_W = B_KV * HEAD_DIM
QKV_W = A_Q_W + 2 * A_KV_W + B_Q_W + 2 * B_KV_W
CZ_W = 2 * C_CH
COL_KA = A_Q_W
COL_VA = COL_KA + A_KV_W
COL_QB = COL_VA + A_KV_W
COL_KB = COL_QB + B_Q_W
COL_VB = COL_KB + B_KV_W

V7X_VMEM_BYTES = 64 * 1024 * 1024
MASK_VALUE = -1e30
LOG2_E = 1.4426950408889634
CONV_HALO = 16
CONV_ROWS = 128
CONV_UNROLL = 2
WIN_LOOKAHEAD = 4
ATTN_QBLK = 256
ATTN_LOOKAHEAD = 14
SUBLANES = 8
BF16_ROWS = 16
VT_ROWS = HEAD_DIM + BF16_ROWS
FFN_TILE = 1408


def _cparams(semantics, vmem_mb):
    assert vmem_mb * 1024 * 1024 < V7X_VMEM_BYTES
    return pltpu.CompilerParams(dimension_semantics=semantics,
                                vmem_limit_bytes=vmem_mb * 1024 * 1024)


def _rms(x, g):
    return x * lax.rsqrt(jnp.mean(x * x, axis=-1, keepdims=True) + EPS) * g


def _qk_t(q, k):
    return lax.dot_general(q, k, (((1,), (1,)), ((), ())), preferred_element_type=F32)


def _resident_layer_spec(stack, layer):
    _, rows, cols = stack.shape
    return pl.BlockSpec((None, rows, cols), lambda *_: (layer, 0, 0), pipeline_mode=pl.Buffered(1))


def _token_tiling(n_tokens, seq_len, n_mod_rows, tm):
    span = n_tokens if n_mod_rows == 1 else seq_len
    tm = min(tm, span)
    assert span % tm == 0
    return tm, span // tm


def _mods_kernel(c_ref, w_ref, b_ref, o_ref):
    a = jax.nn.silu(c_ref[...]).astype(BF16)
    o_ref[0] = jnp.dot(a, w_ref[0].astype(BF16), preferred_element_type=F32) + b_ref[0]


def _ada_mods(cvecs, w_ada, b_ada):
    depth, d, n = w_ada.shape
    r = cvecs.shape[0]
    tn = 1024
    return pl.pallas_call(
        _mods_kernel,
        grid=(depth, n // tn),
        in_specs=[pl.BlockSpec((r, d), lambda l, j: (0, 0)),
                  pl.BlockSpec((1, d, tn), lambda l, j: (l, 0, j)),
                  pl.BlockSpec((1, 1, tn), lambda l, j: (l, 0, j))],
        out_specs=pl.BlockSpec((1, r, tn), lambda l, j: (l, 0, j)),
        out_shape=jax.ShapeDtypeStruct((depth, r, n), F32),
        compiler_params=_cparams(("arbitrary", "arbitrary"), 40),
        name="ada_mods",
    )(cvecs, w_ada, b_ada.reshape(depth, 1, n))


def _inproj_kernel(*refs, rope, emit_kv, emit_vt):
    x_ref, mod_ref, g1_ref, w_ref, gains_ref = refs[:5]
    refs = refs[5:]
    if rope:
        cos_ref, sin_ref = refs[:2]
        refs = refs[2:]
    z_ref, cz_ref = refs[:2]
    refs = refs[2:]
    if emit_kv:
        ka_ref, va_ref, kb_ref, vb_ref = refs[:4]
        refs = refs[4:]
    if emit_vt:
        vt_ref, = refs

    h = _rms(x_ref[...], g1_ref[...])
    h = h * (1.0 + mod_ref[0, 1:2, :]) + mod_ref[0, 0:1, :]
    hb = h.astype(BF16)
    if rope:
        cos = cos_ref[...]
        sin = sin_ref[...]
    q_scale = HEAD_DIM ** -0.5
    q_scale_b = q_scale * LOG2_E

    def normed(zh, gain_idx):
        return _rms(zh, gains_ref[gain_idx:gain_idx + 1, :])

    def store_vt(kv_head, zh):
        r0 = kv_head * VT_ROWS
        vt_ref[r0:r0 + HEAD_DIM, :] = zh.T.astype(BF16)
        vt_ref[r0 + HEAD_DIM:r0 + VT_ROWS, :] = jnp.ones((BF16_ROWS, zh.shape[0]), BF16)

    def rotated(zh):
        if not rope:
            return zh
        return zh * cos + pltpu.roll(zh, HEAD_DIM // 2, axis=1) * sin

    chunk = 512
    for c0 in range(0, QKV_W, chunk):
        zc = jnp.dot(hb, w_ref[:, c0:c0 + chunk], preferred_element_type=F32)
        for j in range(chunk // HEAD_DIM):
            col = c0 + j * HEAD_DIM
            zh = zc[:, j * HEAD_DIM:(j + 1) * HEAD_DIM]
            if col < COL_KA:
                out = rotated(normed(zh, 0)) * q_scale
            elif col < COL_VA:
                kn = normed(zh, 1)
                if emit_kv:
                    ka_ref[:, col - COL_KA:col - COL_KA + HEAD_DIM] = kn
                out = rotated(kn)
            elif col < COL_QB:
                if emit_kv:
                    va_ref[:, col - COL_VA:col - COL_VA + HEAD_DIM] = zh
                if emit_vt:
                    store_vt((col - COL_VA) // HEAD_DIM, zh)
                out = zh
            elif col < COL_KB:
                out = rotated(normed(zh, 2)) * q_scale_b
            elif col < COL_VB:
                kn = normed(zh, 3)
                if emit_kv:
                    kb_ref[:, col - COL_KB:col - COL_KB + HEAD_DIM] = kn
                out = rotated(kn)
            else:
                if emit_kv:
                    vb_ref[:, col - COL_VB:col - COL_VB + HEAD_DIM] = zh
                if emit_vt:
                    store_vt(A_KV + (col - COL_VB) // HEAD_DIM, zh)
                out = zh
            z_ref[:, col:col + HEAD_DIM] = out.astype(BF16)
    for c0 in range(0, CZ_W, chunk):
        cz_ref[:, c0:c0 + chunk] = jnp.dot(hb, w_ref[:, QKV_W + c0:QKV_W + c0 + chunk],
                                           preferred_element_type=F32)


def _in_projection(x, mod, g1, w_in, layer, gains, rope_tabs, seq_len, tm):
    n, d = x.shape
    tm, tiles_per_seq = _token_tiling(n, seq_len, mod.shape[0], tm)
    mod_map = lambda i: (i // tiles_per_seq, 0, 0)
    const = lambda i: (0, 0)
    in_specs = [pl.BlockSpec((tm, d), lambda i: (i, 0)),
                pl.BlockSpec((1, N_MOD, d), mod_map),
                pl.BlockSpec((1, d), const),
                _resident_layer_spec(w_in, layer),
                pl.BlockSpec(gains.shape, const)]
    args = [x, mod, g1, w_in, gains]
    rope = rope_tabs is not None
    emit_kv, emit_vt = not rope, rope
    if rope:
        in_specs += [pl.BlockSpec((tm, HEAD_DIM), lambda i: (i % tiles_per_seq, 0))] * 2
        args += list(rope_tabs)
    out_specs = [pl.BlockSpec((tm, QKV_W), lambda i: (i, 0)),
                 pl.BlockSpec((tm, CZ_W), lambda i: (i, 0))]
    out_shape = [jax.ShapeDtypeStruct((n, QKV_W), BF16), jax.ShapeDtypeStruct((n, CZ_W), F32)]
    if emit_kv:
        out_specs += [pl.BlockSpec((tm, A_KV_W), lambda i: (i, 0))] * 4
        out_shape += [jax.ShapeDtypeStruct((n, A_KV_W), F32)] * 4
    if emit_vt:
        out_specs += [pl.BlockSpec(((A_KV + B_KV) * VT_ROWS, tm), lambda i: (0, i))]
        out_shape += [jax.ShapeDtypeStruct(((A_KV + B_KV) * VT_ROWS, n), BF16)]
    return pl.pallas_call(
        functools.partial(_inproj_kernel, rope=rope, emit_kv=emit_kv, emit_vt=emit_vt),
        grid=(n // tm,),
        in_specs=in_specs,
        out_specs=out_specs,
        out_shape=out_shape,
        compiler_params=_cparams(("arbitrary",), 56),
        name="in_projection_lat" if rope else "in_projection_ctx",
    )(*args)


def _softmax_pv(s_list, v_list, sink_col, exp=jnp.exp):
    m = s_list[0].max(axis=-1, keepdims=True)
    for s in s_list[1:]:
        m = jnp.maximum(m, s.max(axis=-1, keepdims=True))
    if sink_col is not None:
        m = jnp.maximum(m, sink_col)
    denom = None
    acc = None
    for s, v in zip(s_list, v_list):
        p = exp(s - m)
        ps = p.sum(axis=-1, keepdims=True)
        pv = jnp.dot(p.astype(BF16), v, preferred_element_type=F32)
        denom = ps if denom is None else denom + ps
        acc = pv if acc is None else acc + pv
    if sink_col is not None:
        denom = denom + exp(sink_col - m)
    return acc * (1.0 / denom)


def _stack_heads(ref, col0, n_heads):
    return jnp.concatenate([ref[:, col0 + g * HEAD_DIM:col0 + (g + 1) * HEAD_DIM]
                            for g in range(n_heads)], axis=0)


def _sink_column(sink_ref, head0, n_heads, rows_per_head):
    row = lax.broadcasted_iota(jnp.int32, (n_heads * rows_per_head, 1), 0)
    col = jnp.full((n_heads * rows_per_head, 1), sink_ref[head0], F32)
    for g in range(1, n_heads):
        col = jnp.where(row >= g * rows_per_head, sink_ref[head0 + g], col)
    return col


def _ctx_attn_kernel(sink_ref, z_ref, oa_ref, ob_ref):
    t = z_ref.shape[0]
    ga = A_HEADS // A_KV
    for kh in range(A_KV):
        q = _stack_heads(z_ref, kh * ga * HEAD_DIM, ga)
        k = z_ref[:, COL_KA + kh * HEAD_DIM:COL_KA + (kh + 1) * HEAD_DIM]
        v = z_ref[:, COL_VA + kh * HEAD_DIM:COL_VA + (kh + 1) * HEAD_DIM]
        o = _softmax_pv([_qk_t(q, k)], [v], _sink_column(sink_ref, kh * ga, ga, t))
        for g in range(ga):
            c = (kh * ga + g) * HEAD_DIM
            oa_ref[:, c:c + HEAD_DIM] = o[g * t:(g + 1) * t].astype(BF16)
    gb = B_HEADS // B_KV
    for kh in range(B_KV):
        q = _stack_heads(z_ref, COL_QB + kh * gb * HEAD_DIM, gb)
        k = z_ref[:, COL_KB + kh * HEAD_DIM:COL_KB + (kh + 1) * HEAD_DIM]
        v = z_ref[:, COL_VB + kh * HEAD_DIM:COL_VB + (kh + 1) * HEAD_DIM]
        o = _softmax_pv([_qk_t(q, k)], [v], None, exp=jnp.exp2)
        for g in range(gb):
            c = (kh * gb + g) * HEAD_DIM
            ob_ref[:, c:c + HEAD_DIM] = o[g * t:(g + 1) * t].astype(BF16)


def _ctx_attention(z, sink, seq_len):
    n = z.shape[0]
    return pl.pallas_call(
        _ctx_attn_kernel,
        grid=(n // seq_len,),
        in_specs=[pl.BlockSpec(memory_space=pltpu.SMEM),
                  pl.BlockSpec((seq_len, QKV_W), lambda b: (b, 0))],
        out_specs=[pl.BlockSpec((seq_len, A_Q_W), lambda b: (b, 0)),
                   pl.BlockSpec((seq_len, B_Q_W), lambda b: (b, 0))],
        out_shape=[jax.ShapeDtypeStruct((n, A_Q_W), BF16), jax.ShapeDtypeStruct((n, B_Q_W), BF16)],
        compiler_params=_cparams(("arbitrary",), 32),
        name="ctx_attention",
    )(sink, z)


def _win_attn_kernel(sink_ref, q_ref, kp_ref, kc_ref, kn_ref, vtp_ref, vtc_ref, vtn_ref,
                     kx_ref, vxt_ref, oa_ref, *, seq_len):
    tq = q_ref.shape[0]
    t0 = pl.program_id(1) * tq
    ga = A_HEADS // A_KV
    nsub = tq // WINDOW
    w = WINDOW
    key = lax.broadcasted_iota(jnp.int32, (w, ga * w), 0)
    qry = lax.broadcasted_iota(jnp.int32, (w, ga * w), 1) & (w - 1)
    lane = lax.broadcasted_iota(jnp.int32, (1, ga * w), 1)
    prev_mask = key >= qry
    next_mask = key <= qry
    tiles = [(kh, i) for kh in range(A_KV) for i in range(nsub)]

    def rows(i, prev_ref, cur_ref, next_ref, hs, axis):
        def cur(j0, j1):
            sl = slice(j0 * w, j1 * w)
            return cur_ref[sl, hs] if axis == 0 else cur_ref[hs, sl]
        halo = lambda ref: ref[:, hs] if axis == 0 else ref[hs, :]
        lo, hi = max(i - 1, 0), min(i + 2, nsub)
        parts = ([halo(prev_ref)] if i == 0 else []) + [cur(lo, hi)] + (
            [halo(next_ref)] if i == nsub - 1 else [])
        return parts[0] if len(parts) == 1 else jnp.concatenate(parts, axis=axis)

    def scores(kh, i):
        hs = slice(kh * HEAD_DIM, (kh + 1) * HEAD_DIM)
        q = jnp.concatenate([q_ref[i * w:(i + 1) * w, (kh * ga + g) * HEAD_DIM:(kh * ga + g + 1) * HEAD_DIM]
                             for g in range(ga)], axis=0)
        return _qk_t(rows(i, kp_ref, kc_ref, kn_ref, hs, 0), q), _qk_t(kx_ref[0, :, hs], q)

    def finish(kh, i, st_win, st_ctx):
        vs = slice(kh * VT_ROWS, (kh + 1) * VT_ROWS)
        prev_ok = prev_mask if i > 0 else prev_mask & (t0 > 0)
        next_ok = next_mask if i < nsub - 1 else next_mask & (t0 + tq < seq_len)
        parts = [jnp.where(prev_ok, st_win[0:w], MASK_VALUE), st_win[w:2 * w],
                 jnp.where(next_ok, st_win[2 * w:3 * w], MASK_VALUE)]
        sink = jnp.full((1, ga * w), sink_ref[kh * ga], F32)
        for g in range(1, ga):
            sink = jnp.where(lane >= g * w, sink_ref[kh * ga + g], sink)
        m = jnp.maximum(sink, st_ctx.max(axis=0, keepdims=True))
        for s in parts:
            m = jnp.maximum(m, s.max(axis=0, keepdims=True))
        pt_win = jnp.concatenate([jnp.exp(s - m) for s in parts], axis=0).astype(BF16)
        pt_ctx = jnp.exp(st_ctx - m).astype(BF16)
        acc = jnp.dot(rows(i, vtp_ref, vtc_ref, vtn_ref, vs, 1), pt_win, preferred_element_type=F32)
        acc = acc + jnp.dot(vxt_ref[0, vs, :], pt_ctx, preferred_element_type=F32)
        denom = acc[HEAD_DIM:HEAD_DIM + 1] + jnp.exp(sink - m)
        o = (acc[:HEAD_DIM] * (1.0 / denom)).T
        for g in range(ga):
            c = (kh * ga + g) * HEAD_DIM
            oa_ref[i * w:(i + 1) * w, c:c + HEAD_DIM] = o[g * w:(g + 1) * w].astype(BF16)

    pending = [scores(*tiles[t]) for t in range(min(WIN_LOOKAHEAD, len(tiles)))]
    for t, (kh, i) in enumerate(tiles):
        if t + WIN_LOOKAHEAD < len(tiles):
            pending.append(scores(*tiles[t + WIN_LOOKAHEAD]))
        finish(kh, i, *pending.pop(0))


def _win_attention(z, vt, kx, vxt, sink, seq_len, tq):
    n = z.shape[0]
    nb = n // seq_len
    tq = min(tq, seq_len)
    assert tq % WINDOW == 0
    nt = seq_len // tq
    halos_per_tile = tq // WINDOW
    n_halo = n // WINDOW
    past = kx.shape[1]
    vt_rows = A_KV * VT_ROWS
    tile = lambda b, t: b * nt + t
    prev_halo = lambda b, t: jnp.maximum(tile(b, t) * halos_per_tile - 1, 0)
    next_halo = lambda b, t: jnp.minimum((tile(b, t) + 1) * halos_per_tile, n_halo - 1)
    kcb = COL_KA // A_KV_W
    return pl.pallas_call(
        functools.partial(_win_attn_kernel, seq_len=seq_len),
        grid=(nb, nt),
        in_specs=[pl.BlockSpec(memory_space=pltpu.SMEM),
                  pl.BlockSpec((tq, A_Q_W), lambda b, t: (tile(b, t), 0)),
                  pl.BlockSpec((WINDOW, A_KV_W), lambda b, t: (prev_halo(b, t), kcb)),
                  pl.BlockSpec((tq, A_KV_W), lambda b, t: (tile(b, t), kcb)),
                  pl.BlockSpec((WINDOW, A_KV_W), lambda b, t: (next_halo(b, t), kcb)),
                  pl.BlockSpec((vt_rows, WINDOW), lambda b, t: (0, prev_halo(b, t))),
                  pl.BlockSpec((vt_rows, tq), lambda b, t: (0, tile(b, t))),
                  pl.BlockSpec((vt_rows, WINDOW), lambda b, t: (0, next_halo(b, t))),
                  pl.BlockSpec((1, past, A_KV_W), lambda b, t: (b, 0, 0)),
                  pl.BlockSpec((1, vt_rows, past), lambda b, t: (b, 0, 0))],
        out_specs=pl.BlockSpec((tq, A_Q_W), lambda b, t: (tile(b, t), 0)),
        out_shape=jax.ShapeDtypeStruct((n, A_Q_W), BF16),
        compiler_params=_cparams(("arbitrary", "arbitrary"), 40),
        name="win_attention",
    )(sink, z, z, z, z, vt, vt, vt, kx, vxt)


def _glob_attn_kernel(q_ref, kx_ref, vxt_ref, kl_ref, vlt_ref, ob_ref, *, tk):
    gb = B_HEADS // B_KV
    past = kx_ref.shape[1]
    assert past % tk == 0 and kl_ref.shape[0] % tk == 0
    blocks = [(r0, g) for r0 in range(0, q_ref.shape[0], ATTN_QBLK) for g in range(gb)]
    nq = len(blocks)
    queries = [q_ref[r0:r0 + ATTN_QBLK, g * HEAD_DIM:(g + 1) * HEAD_DIM] for r0, g in blocks]

    def update(st, vt, carry):
        c_max = st.max(axis=0, keepdims=True)
        if carry is None:
            m = c_max
            pt = jnp.exp2(st - m)
            return m, jnp.dot(vt, pt.astype(BF16), preferred_element_type=F32)
        m_old, acc = carry
        m = jnp.maximum(m_old, c_max)
        alpha = jnp.exp2(m_old - m)
        pt = jnp.exp2(st - m)
        acc = alpha * acc + jnp.dot(vt, pt.astype(BF16), preferred_element_type=F32)
        return m, acc

    def keys(i):
        c0 = i * tk
        if c0 < past:
            return kx_ref[0, c0:c0 + tk, :], vxt_ref[0, :, c0:c0 + tk]
        c0 -= past
        return kl_ref[c0:c0 + tk, :], vlt_ref[:, c0:c0 + tk]

    n_tiles = (past + kl_ref.shape[0]) // tk * nq
    score = lambda t: _qk_t(keys(t // nq)[0], queries[t % nq])
    carries = [None] * nq
    pending = [score(t) for t in range(min(ATTN_LOOKAHEAD, n_tiles))]
    for t in range(n_tiles):
        if t + ATTN_LOOKAHEAD < n_tiles:
            pending.append(score(t + ATTN_LOOKAHEAD))
        carries[t % nq] = update(pending.pop(0), keys(t // nq)[1], carries[t % nq])
    for (r0, g), (_, acc) in zip(blocks, carries):
        o = acc[:HEAD_DIM] * (1.0 / acc[HEAD_DIM:HEAD_DIM + 1])
        ob_ref[r0:r0 + ATTN_QBLK, g * HEAD_DIM:(g + 1) * HEAD_DIM] = o.T.astype(BF16)


def _glob_attention(z, vt, kx, vxt, seq_len, tq, tk):
    n = z.shape[0]
    nb = n // seq_len
    tq = min(tq, seq_len)
    tk = min(tk, seq_len)
    nt = seq_len // tq
    past = kx.shape[1]
    q_w = B_Q_W // B_KV
    return pl.pallas_call(
        functools.partial(_glob_attn_kernel, tk=tk),
        grid=(nb, B_KV, nt),
        in_specs=[pl.BlockSpec((tq, q_w), lambda b, kh, t: (b * nt + t, COL_QB // q_w + kh)),
                  pl.BlockSpec((1, past, HEAD_DIM), lambda b, kh, t: (b, 0, kh)),
                  pl.BlockSpec((1, VT_ROWS, past), lambda b, kh, t: (b, kh, 0)),
                  pl.BlockSpec((seq_len, HEAD_DIM), lambda b, kh, t: (b, COL_KB // HEAD_DIM + kh)),
                  pl.BlockSpec((VT_ROWS, seq_len), lambda b, kh, t: (A_KV + kh, b))],
        out_specs=pl.BlockSpec((tq, q_w), lambda b, kh, t: (b * nt + t, kh)),
        out_shape=jax.ShapeDtypeStruct((n, B_Q_W), BF16),
        compiler_params=_cparams(("arbitrary", "arbitrary", "arbitrary"), 48),
        name="glob_attention",
    )(z, kx, vxt, z, vt)


def _conv_kernel(cp_ref, cc_ref, cn_ref, w_ref, b_ref, g_ref, beta_ref, o_ref, u_ref, *,
                 tiles_per_seq):
    tc = cc_ref.shape[0]
    ti = pl.program_id(0) % tiles_per_seq

    def glu(ref):
        return ref[:, :C_CH] * jax.nn.sigmoid(ref[:, C_CH:])

    u_ref[0:CONV_HALO, :] = jnp.where(ti > 0, glu(cp_ref), 0.0)
    u_ref[CONV_HALO:CONV_HALO + tc, :] = glu(cc_ref)
    u_ref[CONV_HALO + tc:, :] = jnp.where(ti < tiles_per_seq - 1, glu(cn_ref), 0.0)

    bias = b_ref[...]
    gamma = g_ref[...]
    beta = beta_ref[...]
    base = CONV_HALO - CONV_W // 2

    def body(r, carry):
        r0 = pl.multiple_of(r * CONV_ROWS, CONV_ROWS)
        y = None
        for b in range(SUBLANES):
            v = None
            for a in range((base + CONV_W - 1 - b) // SUBLANES + 1):
                k = SUBLANES * a + b - base
                if k < 0:
                    continue
                term = w_ref[k:k + 1, :] * u_ref[pl.ds(r0 + SUBLANES * a, CONV_ROWS + SUBLANES), :]
                v = term if v is None else v + term
            v = v[b:b + CONV_ROWS, :]
            y = v if y is None else y + v
        y = y + bias
        mu = jnp.mean(y, axis=-1, keepdims=True)
        yc = y - mu
        yn = yc * lax.rsqrt(jnp.mean(yc * yc, axis=-1, keepdims=True) + EPS) * gamma + beta
        o_ref[pl.ds(r0, CONV_ROWS), :] = jax.nn.silu(yn).astype(BF16)
        return carry

    lax.fori_loop(0, tc // CONV_ROWS, body, 0, unroll=CONV_UNROLL)


def _conformer_conv(cz, conv_w, conv_b, ln_g, ln_b, seq_len, tc):
    n = cz.shape[0]
    tc = min(tc, seq_len)
    tiles_per_seq = seq_len // tc
    halos_per_tile = tc // CONV_HALO
    n_halo = n // CONV_HALO
    const = lambda i: (0, 0)
    return pl.pallas_call(
        functools.partial(_conv_kernel, tiles_per_seq=tiles_per_seq),
        grid=(n // tc,),
        in_specs=[pl.BlockSpec((CONV_HALO, CZ_W),
                               lambda i: (jnp.maximum(i * halos_per_tile - 1, 0), 0)),
                  pl.BlockSpec((tc, CZ_W), lambda i: (i, 0)),
                  pl.BlockSpec((CONV_HALO, CZ_W),
                               lambda i: (jnp.minimum((i + 1) * halos_per_tile, n_halo - 1), 0)),
                  pl.BlockSpec((CONV_W, C_CH), const),
                  pl.BlockSpec((1, C_CH), const),
                  pl.BlockSpec((1, C_CH), const),
                  pl.BlockSpec((1, C_CH), const)],
        out_specs=pl.BlockSpec((tc, C_CH), lambda i: (i, 0)),
        out_shape=jax.ShapeDtypeStruct((n, C_CH), BF16),
        scratch_shapes=[pltpu.VMEM((tc + 2 * CONV_HALO, C_CH), F32)],
        compiler_params=_cparams(("arbitrary",), 32),
        name="conformer_conv",
    )(cz, cz, cz, conv_w, conv_b, ln_g, ln_b)


def _outproj_kernel(oa_ref, ob_ref, oc_ref, x_ref, mod_ref, g2_ref, w_ref, xo_ref, h2_ref):
    mix = jnp.concatenate([oa_ref[...], ob_ref[...], oc_ref[...]], axis=1)
    y = jnp.dot(mix, w_ref[...], preferred_element_type=F32)
    x = x_ref[...] + mod_ref[0, 2:3, :] * y
    xo_ref[...] = x
    h2 = _rms(x, g2_ref[...])
    h2_ref[...] = (h2 * (1.0 + mod_ref[0, 4:5, :]) + mod_ref[0, 3:4, :]).astype(BF16)


def _out_projection(oa, ob, oc, x, mod, g2, w_out, layer, seq_len, tm):
    n, d = x.shape
    tm, tiles_per_seq = _token_tiling(n, seq_len, mod.shape[0], tm)
    mod_map = lambda i: (i // tiles_per_seq, 0, 0)
    const = lambda i: (0, 0)
    row = lambda w: pl.BlockSpec((tm, w), lambda i: (i, 0))
    return pl.pallas_call(
        _outproj_kernel,
        grid=(n // tm,),
        in_specs=[row(A_Q_W), row(B_Q_W), row(C_CH), row(d),
                  pl.BlockSpec((1, N_MOD, d), mod_map),
                  pl.BlockSpec((1, d), const),
                  _resident_layer_spec(w_out, layer)],
        out_specs=[row(d), row(d)],
        out_shape=[jax.ShapeDtypeStruct((n, d), F32), jax.ShapeDtypeStruct((n, d), BF16)],
        compiler_params=_cparams(("arbitrary",), 56),
        name="out_projection",
    )(oa, ob, oc, x, mod, g2, w_out)


def _ffn_gate_kernel(h_ref, wgu_ref, a_ref):
    tf = a_ref.shape[1]
    gu = jnp.dot(h_ref[...], wgu_ref[...], preferred_element_type=F32)
    a_ref[...] = (jax.nn.silu(gu[:, :tf]) * gu[:, tf:]).astype(BF16)


def _ffn_down_kernel(a_ref, wd_ref, x_ref, mod_ref, o_ref):
    y = jnp.dot(a_ref[...], wd_ref[...], preferred_element_type=F32)
    o_ref[...] = x_ref[...] + mod_ref[0, 5:6, :] * y


def _ffn(h2, w_gate_up, w_down, layer, x, mod, seq_len, tm_gate, tm_down):
    n, d = x.shape
    d_ff = w_down.shape[1]
    tf = FFN_TILE
    tm_gate = min(tm_gate, n)
    a = pl.pallas_call(
        _ffn_gate_kernel,
        grid=(n // tm_gate, d_ff // tf),
        in_specs=[pl.BlockSpec((tm_gate, d), lambda i, f: (i, 0)),
                  pl.BlockSpec((None, d, 2 * tf), lambda i, f: (layer, 0, f))],
        out_specs=pl.BlockSpec((tm_gate, tf), lambda i, f: (i, f)),
        out_shape=jax.ShapeDtypeStruct((n, d_ff), BF16),
        compiler_params=_cparams(("arbitrary", "arbitrary"), 56),
        name="ffn_gate_up",
    )(h2, w_gate_up)
    tm, tiles_per_seq = _token_tiling(n, seq_len, mod.shape[0], tm_down)
    return pl.pallas_call(
        _ffn_down_kernel,
        grid=(n // tm,),
        in_specs=[pl.BlockSpec((tm, d_ff), lambda i: (i, 0)),
                  _resident_layer_spec(w_down, layer),
                  pl.BlockSpec((tm, d), lambda i: (i, 0)),
                  pl.BlockSpec((1, N_MOD, d), lambda i: (i // tiles_per_seq, 0, 0))],
        out_specs=pl.BlockSpec((tm, d), lambda i: (i, 0)),
        out_shape=jax.ShapeDtypeStruct((n, d), F32),
        compiler_params=_cparams(("arbitrary",), 56),
        name="ffn_down",
    )(a, w_down, x, mod)


def _rope_tables(n_tokens):
    pairs = HEAD_DIM // 4
    t = jnp.arange(n_tokens)
    row = (t // GRID_W).astype(F32)
    col = (t % GRID_W).astype(F32)
    inv = jnp.power(ROPE_BASE, -jnp.arange(pairs, dtype=F32) / pairs)
    ang = jnp.concatenate([row[:, None] * inv, col[:, None] * inv], axis=-1)
    cos = jnp.cos(ang)
    sin = jnp.sin(ang)
    return jnp.concatenate([cos, cos], axis=-1), jnp.concatenate([-sin, sin], axis=-1)


def _interleave_gate_up(w_gate, w_up):
    depth, d, d_ff = w_gate.shape
    assert d_ff % FFN_TILE == 0
    parts = []
    for c0 in range(0, d_ff, FFN_TILE):
        parts += [w_gate[..., c0:c0 + FFN_TILE], w_up[..., c0:c0 + FFN_TILE]]
    return jnp.concatenate(parts, axis=-1).astype(BF16)


def _values_t_with_ones(cache_v):
    b, depth, past, n_kv, d = cache_v.shape
    vt = jnp.transpose(cache_v.astype(BF16), (0, 1, 3, 4, 2))
    ones = jnp.ones((b, depth, n_kv, VT_ROWS - d, past), BF16)
    return jnp.concatenate([vt, ones], axis=3).reshape(b, depth, n_kv * VT_ROWS, past)


def _trunk_layer(x, mod, lw, layer, seq_len, rope_tabs=None, ctx_kv=None):
    (w_in, w_out, w_gate_up, w_down, g1, g2, gains, sink, conv_w, conv_b, cln_g, cln_b) = lw
    latent = ctx_kv is not None
    proj = _in_projection(x, mod, g1, w_in, layer, gains, rope_tabs, seq_len, tm=512)
    z, cz = proj[0], proj[1]
    if latent:
        ka_c, va_c, kb_c, vb_c = ctx_kv
        oa = _win_attention(z, proj[2], ka_c, va_c, sink, seq_len, tq=512)
        ob = _glob_attention(z, proj[2], kb_c, vb_c, seq_len, tq=512, tk=256)
    else:
        oa, ob = _ctx_attention(z, sink, seq_len)
    oc = _conformer_conv(cz, conv_w, conv_b, cln_g, cln_b, seq_len, tc=512)
    x, h2 = _out_projection(oa, ob, oc, x, mod, g2, w_out, layer, seq_len, tm=512)
    x = _ffn(h2, w_gate_up, w_down, layer, x, mod, seq_len, tm_gate=1024, tm_down=512)
    return x, proj[2:]


def kernel(x_prompt, x_sample, cache_a_k, cache_a_v, cache_b_k, cache_b_v, c, c_ctx, w_ada, b_ada, w_in, w_out, w_gate, w_up, w_down, norm1_g, norm2_g, qnorm_a_g, knorm_a_g, qnorm_b_g, knorm_b_g, sink_a, conv_w, conv_b, conv_ln_g, conv_ln_b):
    batch, seq, d = x_prompt.shape
    dec_batch, dec_seq, _ = x_sample.shape
    depth = w_in.shape[0]
    past = cache_a_k.shape[2]

    n_rows = 1 + dec_batch
    pad_rows = -n_rows % 8
    cvecs = jnp.concatenate([c_ctx[None, :], c, jnp.zeros((pad_rows, d), F32)], axis=0)
    mods = _ada_mods(cvecs, w_ada, b_ada).reshape(depth, n_rows + pad_rows, N_MOD, d)

    rope_tabs = _rope_tables(dec_seq)
    w_in_b, w_out_b = w_in.astype(BF16), w_out.astype(BF16)
    w_gate_up_b, w_down_b = _interleave_gate_up(w_gate, w_up), w_down.astype(BF16)
    ctx_kv = [t.astype(BF16).reshape(dec_batch, depth, past, -1)
              for t in (cache_a_k, cache_a_v, cache_b_k, cache_b_v)]
    ctx_kv[1], ctx_kv[3] = _values_t_with_ones(cache_a_v), _values_t_with_ones(cache_b_v)

    y_prompt = x_prompt.reshape(batch * seq, d)
    y_sample = x_sample.reshape(dec_batch * dec_seq, d)
    new_kv = []
    for l in range(depth):
        gains = jnp.stack([qnorm_a_g[l], knorm_a_g[l], qnorm_b_g[l], knorm_b_g[l]])
        lw = (w_in_b, w_out_b, w_gate_up_b, w_down_b,
              norm1_g[l][None, :], norm2_g[l][None, :], gains, sink_a[l],
              conv_w[l], conv_b[l][None, :], conv_ln_g[l][None, :], conv_ln_b[l][None, :])
        y_prompt, kv = _trunk_layer(y_prompt, mods[l, 0:1], lw, l, seq)
        new_kv.append(kv)
        y_sample, _ = _trunk_layer(y_sample, mods[l, 1:n_rows], lw, l, dec_seq, rope_tabs,
                                   [t[:, l] for t in ctx_kv])

    def stacked(i, n_kv):
        return jnp.stack([kv[i].reshape(batch, seq, n_kv, HEAD_DIM) for kv in new_kv], axis=1)

    return (y_prompt.reshape(batch, seq, d), y_sample.reshape(dec_batch, dec_seq, d),
            stacked(0, A_KV), stacked(1, A_KV), stacked(2, B_KV), stacked(3, B_KV))
```

```python
import functools

import jax
import jax.numpy as jnp
from jax import lax
from jax.experimental import pallas as pl
from jax.experimental.pallas import tpu as pltpu

F32 = jnp.float32
BF16 = jnp.bfloat16

HEAD_DIM = 128
GRID_W = 64
A_HEADS = 4
A_KV = 2
B_HEADS = 8
B_KV = 2
C_CH = 512
CONV_W = 31
WINDOW = 128
ROPE_BASE = 10000.0
N_MOD = 6
EPS = 1e-6

A_Q_W = A_HEADS * HEAD_DIM
A_KV_W = A_KV * HEAD_DIM
B_Q_W = B_HEADS * HEAD_DIM
B_KV_W = B_KV * HEAD_DIM
QKV_W = A_Q_W + 2 * A_KV_W + B_Q_W + 2 * B_KV_W
CZ_W = 2 * C_CH
COL_KA = A_Q_W
COL_VA = COL_KA + A_KV_W
COL_QB = COL_VA + A_KV_W
COL_KB = COL_QB + B_Q_W
COL_VB = COL_KB + B_KV_W

V7X_VMEM_BYTES = 64 * 1024 * 1024
MASK_VALUE = -1e30
LOG2_E = 1.4426950408889634
CONV_HALO = 16
CONV_ROWS = 128
CONV_UNROLL = 2
WIN_LOOKAHEAD = 4
ATTN_QBLK = 256
ATTN_LOOKAHEAD = 6
SUBLANES = 8
BF16_ROWS = 16
VT_ROWS = HEAD_DIM + BF16_ROWS
FFN_TILE = 1408


def _cparams(semantics, vmem_mb):
    assert vmem_mb * 1024 * 1024 < V7X_VMEM_BYTES
    return pltpu.CompilerParams(dimension_semantics=semantics,
                                vmem_limit_bytes=vmem_mb * 1024 * 1024)


def _rms(x, g):
    return x * lax.rsqrt(jnp.mean(x * x, axis=-1, keepdims=True) + EPS) * g


def _qk_t(q, k):
    return lax.dot_general(q, k, (((1,), (1,)), ((), ())), preferred_element_type=F32)


def _resident_layer_spec(stack, layer):
    _, rows, cols = stack.shape
    return pl.BlockSpec((None, rows, cols), lambda *_: (layer, 0, 0), pipeline_mode=pl.Buffered(1))


def _token_tiling(n_tokens, seq_len, n_mod_rows, tm):
    span = n_tokens if n_mod_rows == 1 else seq_len
    tm = min(tm, span)
    assert span % tm == 0
    return tm, span // tm


def _mods_kernel(c_ref, w_ref, b_ref, o_ref):
    a = jax.nn.silu(c_ref[...]).astype(BF16)
    o_ref[0] = jnp.dot(a, w_ref[0].astype(BF16), preferred_element_type=F32) + b_ref[0]


def _ada_mods(cvecs, w_ada, b_ada):
    depth, d, n = w_ada.shape
    r = cvecs.shape[0]
    tn = 1024
    return pl.pallas_call(
        _mods_kernel,
        grid=(depth, n // tn),
        in_specs=[pl.BlockSpec((r, d), lambda l, j: (0, 0)),
                  pl.BlockSpec((1, d, tn), lambda l, j: (l, 0, j)),
                  pl.BlockSpec((1, 1, tn), lambda l, j: (l, 0, j))],
        out_specs=pl.BlockSpec((1, r, tn), lambda l, j: (l, 0, j)),
        out_shape=jax.ShapeDtypeStruct((depth, r, n), F32),
        compiler_params=_cparams(("arbitrary", "arbitrary"), 40),
        name="ada_mods",
    )(cvecs, w_ada, b_ada.reshape(depth, 1, n))


def _inproj_kernel(*refs, rope, emit_kv, emit_vt):
    x_ref, mod_ref, g1_ref, w_ref, gains_ref = refs[:5]
    refs = refs[5:]
    if rope:
        cos_ref, sin_ref = refs[:2]
        refs = refs[2:]
    z_ref, cz_ref = refs[:2]
    refs = refs[2:]
    if emit_kv:
        ka_ref, va_ref, kb_ref, vb_ref = refs[:4]
        refs = refs[4:]
    if emit_vt:
        vt_ref, = refs

    h = _rms(x_ref[...], g1_ref[...])
    h = h * (1.0 + mod_ref[0, 1:2, :]) + mod_ref[0, 0:1, :]
    hb = h.astype(BF16)
    if rope:
        cos = cos_ref[...]
        sin = sin_ref[...]
    q_scale = HEAD_DIM ** -0.5
    q_scale_b = q_scale * LOG2_E

    def normed(zh, gain_idx):
        return _rms(zh, gains_ref[gain_idx:gain_idx + 1, :])

    def store_vt(kv_head, zh):
        r0 = kv_head * VT_ROWS
        vt_ref[r0:r0 + HEAD_DIM, :] = zh.T.astype(BF16)
        vt_ref[r0 + HEAD_DIM:r0 + VT_ROWS, :] = jnp.ones((BF16_ROWS, zh.shape[0]), BF16)

    def rotated(zh):
        if not rope:
            return zh
        return zh * cos + pltpu.roll(zh, HEAD_DIM // 2, axis=1) * sin

    chunk = 512
    for c0 in range(0, QKV_W, chunk):
        zc = jnp.dot(hb, w_ref[:, c0:c0 + chunk], preferred_element_type=F32)
        for j in range(chunk // HEAD_DIM):
            col = c0 + j * HEAD_DIM
            zh = zc[:, j * HEAD_DIM:(j + 1) * HEAD_DIM]
            if col < COL_KA:
                out = rotated(normed(zh, 0)) * q_scale
            elif col < COL_VA:
                kn = normed(zh, 1)
                if emit_kv:
                    ka_ref[:, col - COL_KA:col - COL_KA + HEAD_DIM] = kn
                out = rotated(kn)
            elif col < COL_QB:
                if emit_kv:
                    va_ref[:, col - COL_VA:col - COL_VA + HEAD_DIM] = zh
                if emit_vt:
                    store_vt((col - COL_VA) // HEAD_DIM, zh)
                out = zh
            elif col < COL_KB:
                out = rotated(normed(zh, 2)) * q_scale_b
            elif col < COL_VB:
                kn = normed(zh, 3)
                if emit_kv:
                    kb_ref[:, col - COL_KB:col - COL_KB + HEAD_DIM] = kn
                out = rotated(kn)
            else:
                if emit_kv:
                    vb_ref[:, col - COL_VB:col - COL_VB + HEAD_DIM] = zh
                if emit_vt:
                    store_vt(A_KV + (col - COL_VB) // HEAD_DIM, zh)
                out = zh
            z_ref[:, col:col + HEAD_DIM] = out.astype(BF16)
    for c0 in range(0, CZ_W, chunk):
        cz_ref[:, c0:c0 + chunk] = jnp.dot(hb, w_ref[:, QKV_W + c0:QKV_W + c0 + chunk],
                                           preferred_element_type=F32)


def _in_projection(x, mod, g1, w_in, layer, gains, rope_tabs, seq_len, tm):
    n, d = x.shape
    tm, tiles_per_seq = _token_tiling(n, seq_len, mod.shape[0], tm)
    mod_map = lambda i: (i // tiles_per_seq, 0, 0)
    const = lambda i: (0, 0)
    in_specs = [pl.BlockSpec((tm, d), lambda i: (i, 0)),
                pl.BlockSpec((1, N_MOD, d), mod_map),
                pl.BlockSpec((1, d), const),
                _resident_layer_spec(w_in, layer),
                pl.BlockSpec(gains.shape, const)]
    args = [x, mod, g1, w_in, gains]
    rope = rope_tabs is not None
    emit_kv, emit_vt = not rope, rope
    if rope:
        in_specs += [pl.BlockSpec((tm, HEAD_DIM), lambda i: (i % tiles_per_seq, 0))] * 2
        args += list(rope_tabs)
    out_specs = [pl.BlockSpec((tm, QKV_W), lambda i: (i, 0)),
                 pl.BlockSpec((tm, CZ_W), lambda i: (i, 0))]
    out_shape = [jax.ShapeDtypeStruct((n, QKV_W), BF16), jax.ShapeDtypeStruct((n, CZ_W), F32)]
    if emit_kv:
        out_specs += [pl.BlockSpec((tm, A_KV_W), lambda i: (i, 0))] * 4
        out_shape += [jax.ShapeDtypeStruct((n, A_KV_W), F32)] * 4
    if emit_vt:
        out_specs += [pl.BlockSpec(((A_KV + B_KV) * VT_ROWS, tm), lambda i: (0, i))]
        out_shape += [jax.ShapeDtypeStruct(((A_KV + B_KV) * VT_ROWS, n), BF16)]
    return pl.pallas_call(
        functools.partial(_inproj_kernel, rope=rope, emit_kv=emit_kv, emit_vt=emit_vt),
        grid=(n // tm,),
        in_specs=in_specs,
        out_specs=out_specs,
        out_shape=out_shape,
        compiler_params=_cparams(("arbitrary",), 56),
        name="in_projection_lat" if rope else "in_projection_ctx",
    )(*args)


def _softmax_pv(s_list, v_list, sink_col, exp=jnp.exp):
    m = s_list[0].max(axis=-1, keepdims=True)
    for s in s_list[1:]:
        m = jnp.maximum(m, s.max(axis=-1, keepdims=True))
    if sink_col is not None:
        m = jnp.maximum(m, sink_col)
    denom = None
    acc = None
    for s, v in zip(s_list, v_list):
        p = exp(s - m)
        ps = p.sum(axis=-1, keepdims=True)
        pv = jnp.dot(p.astype(BF16), v, preferred_element_type=F32)
        denom = ps if denom is None else denom + ps
        acc = pv if acc is None else acc + pv
    if sink_col is not None:
        denom = denom + exp(sink_col - m)
    return acc * (1.0 / denom)


def _stack_heads(ref, col0, n_heads):
    return jnp.concatenate([ref[:, col0 + g * HEAD_DIM:col0 + (g + 1) * HEAD_DIM]
                            for g in range(n_heads)], axis=0)


def _sink_column(sink_ref, head0, n_heads, rows_per_head):
    row = lax.broadcasted_iota(jnp.int32, (n_heads * rows_per_head, 1), 0)
    col = jnp.full((n_heads * rows_per_head, 1), sink_ref[head0], F32)
    for g in range(1, n_heads):
        col = jnp.where(row >= g * rows_per_head, sink_ref[head0 + g], col)
    return col


def _ctx_attn_kernel(sink_ref, z_ref, oa_ref, ob_ref):
    t = z_ref.shape[0]
    ga = A_HEADS // A_KV
    for kh in range(A_KV):
        q = _stack_heads(z_ref, kh * ga * HEAD_DIM, ga)
        k = z_ref[:, COL_KA + kh * HEAD_DIM:COL_KA + (kh + 1) * HEAD_DIM]
        v = z_ref[:, COL_VA + kh * HEAD_DIM:COL_VA + (kh + 1) * HEAD_DIM]
        o = _softmax_pv([_qk_t(q, k)], [v], _sink_column(sink_ref, kh * ga, ga, t))
        for g in range(ga):
            c = (kh * ga + g) * HEAD_DIM
            oa_ref[:, c:c + HEAD_DIM] = o[g * t:(g + 1) * t].astype(BF16)
    gb = B_HEADS // B_KV
    for kh in range(B_KV):
        q = _stack_heads(z_ref, COL_QB + kh * gb * HEAD_DIM, gb)
        k = z_ref[:, COL_KB + kh * HEAD_DIM:COL_KB + (kh + 1) * HEAD_DIM]
        v = z_ref[:, COL_VB + kh * HEAD_DIM:COL_VB + (kh + 1) * HEAD_DIM]
        o = _softmax_pv([_qk_t(q, k)], [v], None, exp=jnp.exp2)
        for g in range(gb):
            c = (kh * gb + g) * HEAD_DIM
            ob_ref[:, c:c + HEAD_DIM] = o[g * t:(g + 1) * t].astype(BF16)


def _ctx_attention(z, sink, seq_len):
    n = z.shape[0]
    return pl.pallas_call(
        _ctx_attn_kernel,
        grid=(n // seq_len,),
        in_specs=[pl.BlockSpec(memory_space=pltpu.SMEM),
                  pl.BlockSpec((seq_len, QKV_W), lambda b: (b, 0))],
        out_specs=[pl.BlockSpec((seq_len, A_Q_W), lambda b: (b, 0)),
                   pl.BlockSpec((seq_len, B_Q_W), lambda b: (b, 0))],
        out_shape=[jax.ShapeDtypeStruct((n, A_Q_W), BF16), jax.ShapeDtypeStruct((n, B_Q_W), BF16)],
        compiler_params=_cparams(("arbitrary",), 32),
        name="ctx_attention",
    )(sink, z)


def _win_attn_kernel(sink_ref, q_ref, kp_ref, kc_ref, kn_ref, vtp_ref, vtc_ref, vtn_ref,
                     kx_ref, vxt_ref, oa_ref, *, seq_len):
    tq = q_ref.shape[0]
    t0 = pl.program_id(1) * tq
    ga = A_HEADS // A_KV
    nsub = tq // WINDOW
    w = WINDOW
    key = lax.broadcasted_iota(jnp.int32, (w, ga * w), 0)
    qry = lax.broadcasted_iota(jnp.int32, (w, ga * w), 1) & (w - 1)
    lane = lax.broadcasted_iota(jnp.int32, (1, ga * w), 1)
    prev_mask = key >= qry
    next_mask = key <= qry
    tiles = [(kh, i) for kh in range(A_KV) for i in range(nsub)]

    def rows(i, prev_ref, cur_ref, next_ref, hs, axis):
        def cur(j0, j1):
            sl = slice(j0 * w, j1 * w)
            return cur_ref[sl, hs] if axis == 0 else cur_ref[hs, sl]
        halo = lambda ref: ref[:, hs] if axis == 0 else ref[hs, :]
        lo, hi = max(i - 1, 0), min(i + 2, nsub)
        parts = ([halo(prev_ref)] if i == 0 else []) + [cur(lo, hi)] + (
            [halo(next_ref)] if i == nsub - 1 else [])
        return parts[0] if len(parts) == 1 else jnp.concatenate(parts, axis=axis)

    def scores(kh, i):
        hs = slice(kh * HEAD_DIM, (kh + 1) * HEAD_DIM)
        q = jnp.concatenate([q_ref[i * w:(i + 1) * w, (kh * ga + g) * HEAD_DIM:(kh * ga + g + 1) * HEAD_DIM]
                             for g in range(ga)], axis=0)
        return _qk_t(rows(i, kp_ref, kc_ref, kn_ref, hs, 0), q), _qk_t(kx_ref[0, :, hs], q)

    def finish(kh, i, st_win, st_ctx):
        vs = slice(kh * VT_ROWS, (kh + 1) * VT_ROWS)
        prev_ok = prev_mask if i > 0 else prev_mask & (t0 > 0)
        next_ok = next_mask if i < nsub - 1 else next_mask & (t0 + tq < seq_len)
        parts = [jnp.where(prev_ok, st_win[0:w], MASK_VALUE), st_win[w:2 * w],
                 jnp.where(next_ok, st_win[2 * w:3 * w], MASK_VALUE)]
        sink = jnp.full((1, ga * w), sink_ref[kh * ga], F32)
        for g in range(1, ga):
            sink = jnp.where(lane >= g * w, sink_ref[kh * ga + g], sink)
        m = jnp.maximum(sink, st_ctx.max(axis=0, keepdims=True))
        for s in parts:
            m = jnp.maximum(m, s.max(axis=0, keepdims=True))
        pt_win = jnp.concatenate([jnp.exp(s - m) for s in parts], axis=0).astype(BF16)
        pt_ctx = jnp.exp(st_ctx - m).astype(BF16)
        acc = jnp.dot(rows(i, vtp_ref, vtc_ref, vtn_ref, vs, 1), pt_win, preferred_element_type=F32)
        acc = acc + jnp.dot(vxt_ref[0, vs, :], pt_ctx, preferred_element_type=F32)
        denom = acc[HEAD_DIM:HEAD_DIM + 1] + jnp.exp(sink - m)
        o = (acc[:HEAD_DIM] * (1.0 / denom)).T
        for g in range(ga):
            c = (kh * ga + g) * HEAD_DIM
            oa_ref[i * w:(i + 1) * w, c:c + HEAD_DIM] = o[g * w:(g + 1) * w].astype(BF16)

    pending = [scores(*tiles[t]) for t in range(min(WIN_LOOKAHEAD, len(tiles)))]
    for t, (kh, i) in enumerate(tiles):
        if t + WIN_LOOKAHEAD < len(tiles):
            pending.append(scores(*tiles[t + WIN_LOOKAHEAD]))
        finish(kh, i, *pending.pop(0))


def _win_attention(z, vt, kx, vxt, sink, seq_len, tq):
    n = z.shape[0]
    nb = n // seq_len
    tq = min(tq, seq_len)
    assert tq % WINDOW == 0
    nt = seq_len // tq
    halos_per_tile = tq // WINDOW
    n_halo = n // WINDOW
    past = kx.shape[1]
    vt_rows = A_KV * VT_ROWS
    tile = lambda b, t: b * nt + t
    prev_halo = lambda b, t: jnp.maximum(tile(b, t) * halos_per_tile - 1, 0)
    next_halo = lambda b, t: jnp.minimum((tile(b, t) + 1) * halos_per_tile, n_halo - 1)
    kcb = COL_KA // A_KV_W
    return pl.pallas_call(
        functools.partial(_win_attn_kernel, seq_len=seq_len),
        grid=(nb, nt),
        in_specs=[pl.BlockSpec(memory_space=pltpu.SMEM),
                  pl.BlockSpec((tq, A_Q_W), lambda b, t: (tile(b, t), 0)),
                  pl.BlockSpec((WINDOW, A_KV_W), lambda b, t: (prev_halo(b, t), kcb)),
                  pl.BlockSpec((tq, A_KV_W), lambda b, t: (tile(b, t), kcb)),
                  pl.BlockSpec((WINDOW, A_KV_W), lambda b, t: (next_halo(b, t), kcb)),
                  pl.BlockSpec((vt_rows, WINDOW), lambda b, t: (0, prev_halo(b, t))),
                  pl.BlockSpec((vt_rows, tq), lambda b, t: (0, tile(b, t))),
                  pl.BlockSpec((vt_rows, WINDOW), lambda b, t: (0, next_halo(b, t))),
                  pl.BlockSpec((1, past, A_KV_W), lambda b, t: (b, 0, 0)),
                  pl.BlockSpec((1, vt_rows, past), lambda b, t: (b, 0, 0))],
        out_specs=pl.BlockSpec((tq, A_Q_W), lambda b, t: (tile(b, t), 0)),
        out_shape=jax.ShapeDtypeStruct((n, A_Q_W), BF16),
        compiler_params=_cparams(("arbitrary", "arbitrary"), 40),
        name="win_attention",
    )(sink, z, z, z, z, vt, vt, vt, kx, vxt)


def _glob_attn_kernel(q_ref, kx_ref, vxt_ref, kl_ref, vlt_ref, ob_ref, *, tk):
    gb = B_HEADS // B_KV
    past = kx_ref.shape[1]
    assert kl_ref.shape[0] % tk == 0
    blocks = [(r0, g) for r0 in range(0, q_ref.shape[0], ATTN_QBLK) for g in range(gb)]
    nq = len(blocks)
    queries = [q_ref[r0:r0 + ATTN_QBLK, g * HEAD_DIM:(g + 1) * HEAD_DIM] for r0, g in blocks]

    def update(st, vt, carry):
        c_max = st.max(axis=0, keepdims=True)
        if carry is None:
            m = c_max
            pt = jnp.exp2(st - m)
            return m, jnp.dot(vt, pt.astype(BF16), preferred_element_type=F32)
        m_old, acc = carry
        m = jnp.maximum(m_old, c_max)
        alpha = jnp.exp2(m_old - m)
        pt = jnp.exp2(st - m)
        acc = alpha * acc + jnp.dot(vt, pt.astype(BF16), preferred_element_type=F32)
        return m, acc

    chunks = [(True, c0, min(tk, past - c0)) for c0 in range(0, past, tk)]
    chunks += [(False, c0, tk) for c0 in range(0, kl_ref.shape[0], tk)]

    def keys(i):
        ctx, c0, size = chunks[i]
        if ctx:
            return kx_ref[0, c0:c0 + size, :], vxt_ref[0, :, c0:c0 + size]
        return kl_ref[c0:c0 + size, :], vlt_ref[:, c0:c0 + size]

    n_tiles = len(chunks) * nq
    score = lambda t: _qk_t(keys(t // nq)[0], queries[t % nq])
    carries = [None] * nq
    pending = [score(t) for t in range(min(ATTN_LOOKAHEAD, n_tiles))]
    for t in range(n_tiles):
        if t + ATTN_LOOKAHEAD < n_tiles:
            pending.append(score(t + ATTN_LOOKAHEAD))
        carries[t % nq] = update(pending.pop(0), keys(t // nq)[1], carries[t % nq])
    for (r0, g), (_, acc) in zip(blocks, carries):
        o = acc[:HEAD_DIM] * (1.0 / acc[HEAD_DIM:HEAD_DIM + 1])
        ob_ref[r0:r0 + ATTN_QBLK, g * HEAD_DIM:(g + 1) * HEAD_DIM] = o.T.astype(BF16)


def _glob_attention(z, vt, kx, vxt, seq_len, tq, tk):
    n = z.shape[0]
    nb = n // seq_len
    tq = min(tq, seq_len)
    tk = min(tk, seq_len)
    nt = seq_len // tq
    past = kx.shape[1]
    q_w = B_Q_W // B_KV
    return pl.pallas_call(
        functools.partial(_glob_attn_kernel, tk=tk),
        grid=(nb, B_KV, nt),
        in_specs=[pl.BlockSpec((tq, q_w), lambda b, kh, t: (b * nt + t, COL_QB // q_w + kh)),
                  pl.BlockSpec((1, past, HEAD_DIM), lambda b, kh, t: (b, 0, kh)),
                  pl.BlockSpec((1, VT_ROWS, past), lambda b, kh, t: (b, kh, 0)),
                  pl.BlockSpec((seq_len, HEAD_DIM), lambda b, kh, t: (b, COL_KB // HEAD_DIM + kh)),
                  pl.BlockSpec((VT_ROWS, seq_len), lambda b, kh, t: (A_KV + kh, b))],
        out_specs=pl.BlockSpec((tq, q_w), lambda b, kh, t: (b * nt + t, kh)),
        out_shape=jax.ShapeDtypeStruct((n, B_Q_W), BF16),
        compiler_params=_cparams(("arbitrary", "arbitrary", "arbitrary"), 48),
        name="glob_attention",
    )(z, kx, vxt, z, vt)


def _conv_kernel(cp_ref, cc_ref, cn_ref, w_ref, b_ref, g_ref, beta_ref, o_ref, u_ref, *,
                 tiles_per_seq):
    tc = cc_ref.shape[0]
    ti = pl.program_id(0) % tiles_per_seq

    def glu(ref):
        return ref[:, :C_CH] * jax.nn.sigmoid(ref[:, C_CH:])

    u_ref[0:CONV_HALO, :] = jnp.where(ti > 0, glu(cp_ref), 0.0)
    u_ref[CONV_HALO:CONV_HALO + tc, :] = glu(cc_ref)
    u_ref[CONV_HALO + tc:, :] = jnp.where(ti < tiles_per_seq - 1, glu(cn_ref), 0.0)

    bias = b_ref[...]
    gamma = g_ref[...]
    beta = beta_ref[...]
    base = CONV_HALO - CONV_W // 2

    def body(r, carry):
        r0 = pl.multiple_of(r * CONV_ROWS, CONV_ROWS)
        y = None
        for b in range(SUBLANES):
            v = None
            for a in range((base + CONV_W - 1 - b) // SUBLANES + 1):
                k = SUBLANES * a + b - base
                if k < 0:
                    continue
                term = w_ref[k:k + 1, :] * u_ref[pl.ds(r0 + SUBLANES * a, CONV_ROWS + SUBLANES), :]
                v = term if v is None else v + term
            v = v[b:b + CONV_ROWS, :]
            y = v if y is None else y + v
        y = y + bias
        mu = jnp.mean(y, axis=-1, keepdims=True)
        yc = y - mu
        yn = yc * lax.rsqrt(jnp.mean(yc * yc, axis=-1, keepdims=True) + EPS) * gamma + beta
        o_ref[pl.ds(r0, CONV_ROWS), :] = jax.nn.silu(yn).astype(BF16)
        return carry

    lax.fori_loop(0, tc // CONV_ROWS, body, 0, unroll=CONV_UNROLL)


def _conformer_conv(cz, conv_w, conv_b, ln_g, ln_b, seq_len, tc):
    n = cz.shape[0]
    tc = min(tc, seq_len)
    tiles_per_seq = seq_len // tc
    halos_per_tile = tc // CONV_HALO
    n_halo = n // CONV_HALO
    const = lambda i: (0, 0)
    return pl.pallas_call(
        functools.partial(_conv_kernel, tiles_per_seq=tiles_per_seq),
        grid=(n // tc,),
        in_specs=[pl.BlockSpec((CONV_HALO, CZ_W),
                               lambda i: (jnp.maximum(i * halos_per_tile - 1, 0), 0)),
                  pl.BlockSpec((tc, CZ_W), lambda i: (i, 0)),
                  pl.BlockSpec((CONV_HALO, CZ_W),
                               lambda i: (jnp.minimum((i + 1) * halos_per_tile, n_halo - 1), 0)),
                  pl.BlockSpec((CONV_W, C_CH), const),
                  pl.BlockSpec((1, C_CH), const),
                  pl.BlockSpec((1, C_CH), const),
                  pl.BlockSpec((1, C_CH), const)],
        out_specs=pl.BlockSpec((tc, C_CH), lambda i: (i, 0)),
        out_shape=jax.ShapeDtypeStruct((n, C_CH), BF16),
        scratch_shapes=[pltpu.VMEM((tc + 2 * CONV_HALO, C_CH), F32)],
        compiler_params=_cparams(("arbitrary",), 32),
        name="conformer_conv",
    )(cz, cz, cz, conv_w, conv_b, ln_g, ln_b)


def _outproj_kernel(oa_ref, ob_ref, oc_ref, x_ref, mod_ref, g2_ref, w_ref, xo_ref, h2_ref):
    mix = jnp.concatenate([oa_ref[...], ob_ref[...], oc_ref[...]], axis=1)
    y = jnp.dot(mix, w_ref[...], preferred_element_type=F32)
    x = x_ref[...] + mod_ref[0, 2:3, :] * y
    xo_ref[...] = x
    h2 = _rms(x, g2_ref[...])
    h2_ref[...] = (h2 * (1.0 + mod_ref[0, 4:5, :]) + mod_ref[0, 3:4, :]).astype(BF16)


def _out_projection(oa, ob, oc, x, mod, g2, w_out, layer, seq_len, tm):
    n, d = x.shape
    tm, tiles_per_seq = _token_tiling(n, seq_len, mod.shape[0], tm)
    mod_map = lambda i: (i // tiles_per_seq, 0, 0)
    const = lambda i: (0, 0)
    row = lambda w: pl.BlockSpec((tm, w), lambda i: (i, 0))
    return pl.pallas_call(
        _outproj_kernel,
        grid=(n // tm,),
        in_specs=[row(A_Q_W), row(B_Q_W), row(C_CH), row(d),
                  pl.BlockSpec((1, N_MOD, d), mod_map),
                  pl.BlockSpec((1, d), const),
                  _resident_layer_spec(w_out, layer)],
        out_specs=[row(d), row(d)],
        out_shape=[jax.ShapeDtypeStruct((n, d), F32), jax.ShapeDtypeStruct((n, d), BF16)],
        compiler_params=_cparams(("arbitrary",), 56),
        name="out_projection",
    )(oa, ob, oc, x, mod, g2, w_out)


def _ffn_gate_kernel(h_ref, wgu_ref, a_ref):
    tf = a_ref.shape[1]
    gu = jnp.dot(h_ref[...], wgu_ref[...], preferred_element_type=F32)
    a_ref[...] = (jax.nn.silu(gu[:, :tf]) * gu[:, tf:]).astype(BF16)


def _ffn_down_kernel(a_ref, wd_ref, x_ref, mod_ref, o_ref):
    y = jnp.dot(a_ref[...], wd_ref[...], preferred_element_type=F32)
    o_ref[...] = x_ref[...] + mod_ref[0, 5:6, :] * y


def _ffn(h2, w_gate_up, w_down, layer, x, mod, seq_len, tm_gate, tm_down):
    n, d = x.shape
    d_ff = w_down.shape[1]
    tf = FFN_TILE
    tm_gate = min(tm_gate, n)
    a = pl.pallas_call(
        _ffn_gate_kernel,
        grid=(n // tm_gate, d_ff // tf),
        in_specs=[pl.BlockSpec((tm_gate, d), lambda i, f: (i, 0)),
                  pl.BlockSpec((None, d, 2 * tf), lambda i, f: (layer, 0, f))],
        out_specs=pl.BlockSpec((tm_gate, tf), lambda i, f: (i, f)),
        out_shape=jax.ShapeDtypeStruct((n, d_ff), BF16),
        compiler_params=_cparams(("arbitrary", "arbitrary"), 56),
        name="ffn_gate_up",
    )(h2, w_gate_up)
    tm, tiles_per_seq = _token_tiling(n, seq_len, mod.shape[0], tm_down)
    return pl.pallas_call(
        _ffn_down_kernel,
        grid=(n // tm,),
        in_specs=[pl.BlockSpec((tm, d_ff), lambda i: (i, 0)),
                  _resident_layer_spec(w_down, layer),
                  pl.BlockSpec((tm, d), lambda i: (i, 0)),
                  pl.BlockSpec((1, N_MOD, d), lambda i: (i // tiles_per_seq, 0, 0))],
        out_specs=pl.BlockSpec((tm, d), lambda i: (i, 0)),
        out_shape=jax.ShapeDtypeStruct((n, d), F32),
        compiler_params=_cparams(("arbitrary",), 56),
        name="ffn_down",
    )(a, w_down, x, mod)


def _rope_tables(n_tokens):
    pairs = HEAD_DIM // 4
    t = jnp.arange(n_tokens)
    row = (t // GRID_W).astype(F32)
    col = (t % GRID_W).astype(F32)
    inv = jnp.power(ROPE_BASE, -jnp.arange(pairs, dtype=F32) / pairs)
    ang = jnp.concatenate([row[:, None] * inv, col[:, None] * inv], axis=-1)
    cos = jnp.cos(ang)
    sin = jnp.sin(ang)
    return jnp.concatenate([cos, cos], axis=-1), jnp.concatenate([-sin, sin], axis=-1)


def _interleave_gate_up(w_gate, w_up):
    depth, d, d_ff = w_gate.shape
    assert d_ff % FFN_TILE == 0
    parts = []
    for c0 in range(0, d_ff, FFN_TILE):
        parts += [w_gate[..., c0:c0 + FFN_TILE], w_up[..., c0:c0 + FFN_TILE]]
    return jnp.concatenate(parts, axis=-1).astype(BF16)


def _values_t_with_ones(cache_v):
    b, depth, past, n_kv, d = cache_v.shape
    vt = jnp.transpose(cache_v.astype(BF16), (0, 1, 3, 4, 2))
    ones = jnp.ones((b, depth, n_kv, VT_ROWS - d, past), BF16)
    return jnp.concatenate([vt, ones], axis=3).reshape(b, depth, n_kv * VT_ROWS, past)


def _trunk_layer(x, mod, lw, layer, seq_len, rope_tabs=None, ctx_kv=None):
    (w_in, w_out, w_gate_up, w_down, g1, g2, gains, sink, conv_w, conv_b, cln_g, cln_b) = lw
    latent = ctx_kv is not None
    proj = _in_projection(x, mod, g1, w_in, layer, gains, rope_tabs, seq_len, tm=512)
    z, cz = proj[0], proj[1]
    if latent:
        ka_c, va_c, kb_c, vb_c = ctx_kv
        oa = _win_attention(z, proj[2], ka_c, va_c, sink, seq_len, tq=512)
        ob = _glob_attention(z, proj[2], kb_c, vb_c, seq_len, tq=512, tk=512)
    else:
        oa, ob = _ctx_attention(z, sink, seq_len)
    oc = _conformer_conv(cz, conv_w, conv_b, cln_g, cln_b, seq_len, tc=512)
    x, h2 = _out_projection(oa, ob, oc, x, mod, g2, w_out, layer, seq_len, tm=512)
    x = _ffn(h2, w_gate_up, w_down, layer, x, mod, seq_len, tm_gate=1024, tm_down=512)
    return x, proj[2:]


def kernel(x_prompt, x_sample, cache_a_k, cache_a_v, cache_b_k, cache_b_v, c, c_ctx, w_ada, b_ada, w_in, w_out, w_gate, w_up, w_down, norm1_g, norm2_g, qnorm_a_g, knorm_a_g, qnorm_b_g, knorm_b_g, sink_a, conv_w, conv_b, conv_ln_g, conv_ln_b):
    batch, seq, d = x_prompt.shape
    dec_batch, dec_seq, _ = x_sample.shape
    depth = w_in.shape[0]
    past = cache_a_k.shape[2]

    n_rows = 1 + dec_batch
    pad_rows = -n_rows % 8
    cvecs = jnp.concatenate([c_ctx[None, :], c, jnp.zeros((pad_rows, d), F32)], axis=0)
    mods = _ada_mods(cvecs, w_ada, b_ada).reshape(depth, n_rows + pad_rows, N_MOD, d)

    rope_tabs = _rope_tables(dec_seq)
    w_in_b, w_out_b = w_in.astype(BF16), w_out.astype(BF16)
    w_gate_up_b, w_down_b = _interleave_gate_up(w_gate, w_up), w_down.astype(BF16)
    ctx_kv = [t.astype(BF16).reshape(dec_batch, depth, past, -1)
              for t in (cache_a_k, cache_a_v, cache_b_k, cache_b_v)]
    ctx_kv[1], ctx_kv[3] = _values_t_with_ones(cache_a_v), _values_t_with_ones(cache_b_v)

    y_prompt = x_prompt.reshape(batch * seq, d)
    y_sample = x_sample.reshape(dec_batch * dec_seq, d)
    new_kv = []
    for l in range(depth):
        gains = jnp.stack([qnorm_a_g[l], knorm_a_g[l], qnorm_b_g[l], knorm_b_g[l]])
        lw = (w_in_b, w_out_b, w_gate_up_b, w_down_b,
              norm1_g[l][None, :], norm2_g[l][None, :], gains, sink_a[l],
              conv_w[l], conv_b[l][None, :], conv_ln_g[l][None, :], conv_ln_b[l][None, :])
        y_prompt, kv = _trunk_layer(y_prompt, mods[l, 0:1], lw, l, seq)
        new_kv.append(kv)
        y_sample, _ = _trunk_layer(y_sample, mods[l, 1:n_rows], lw, l, dec_seq, rope_tabs,
                                   [t[:, l] for t in ctx_kv])

    def stacked(i, n_kv):
        return jnp.stack([kv[i].reshape(batch, seq, n_kv, HEAD_DIM) for kv in new_kv], axis=1)

    return (y_prompt.reshape(batch, seq, d), y_sample.reshape(dec_batch, dec_seq, d),
            stacked(0, A_KV), stacked(1, A_KV), stacked(2, B_KV), stacked(3, B_KV))
```
